```python
import math
import jax, jax.numpy as jnp
from jax import lax
import numpy as np


D_MODEL = 4096
BATCH = 1
SEQ = 16384
DEPTH = 4

HEAD_DIM = 128
DN_HEADS = 16
DN_DK = 128
DN_DV = 128
DN_QK_W = DN_HEADS * DN_DK
DN_V_W = DN_HEADS * DN_DV
DN_CONV_W = 2 * DN_QK_W + DN_V_W
CONV_K = 4
CHUNK = 64
SWA_Q_HEADS = 16
SWA_KV_HEADS = 4
SWA_GROUP = SWA_Q_HEADS // SWA_KV_HEADS
SWA_Q_W = SWA_Q_HEADS * HEAD_DIM
SWA_KV_W = SWA_KV_HEADS * HEAD_DIM
WINDOW = 128
BLOCK = 128
MIX_WIDTH = DN_V_W + SWA_Q_W
SPLITS = (DN_CONV_W, DN_V_W, DN_HEADS, DN_HEADS, SWA_Q_W, SWA_KV_W, SWA_KV_W)
N_IN = sum(SPLITS)
D_FF = 8192
N_EXPERTS = 8
TOP_K = 2
D_FF_EXPERT = 2048
N_DENSE = (DEPTH + 1) // 2
N_MOE = DEPTH // 2
DN_ALPHA = (2 * DEPTH) ** 0.25
DN_BETA = (8 * DEPTH) ** -0.25
LN_EPS = 1e-5
NEG_INF = -1e30

kernel_name = 'hybrid_deltanet_swa_sink_alibi_deepnorm_moe'


def layer_norm(x, g, b):
    xf = x.astype(jnp.float32)
    mu = jnp.mean(xf, -1, keepdims=True)
    var = jnp.mean(jnp.square(xf - mu), -1, keepdims=True)
    return ((xf - mu) * lax.rsqrt(var + LN_EPS) * g + b).astype(x.dtype)


def l2norm(t):
    return t * lax.rsqrt(jnp.sum(jnp.square(t), -1, keepdims=True) + 1e-6)


def causal_short_conv(u, w):
    c = u.shape[-1]
    return lax.conv_general_dilated(u, w[:, None, :].astype(u.dtype), window_strides=(1,),
                                    padding=[(CONV_K - 1, 0)],
                                    dimension_numbers=('NWC', 'WIO', 'NWC'),
                                    feature_group_count=c)


def gated_delta_rule(q, k, v, g, beta):
    f32 = jnp.float32
    b_, t_, h_, dk = q.shape
    dv = v.shape[-1]
    n = t_ // CHUNK

    def chunks(a):
        return a.astype(f32).reshape(b_, n, CHUNK, h_, -1).transpose(0, 1, 3, 2, 4)

    qc = chunks(q) * (dk ** -0.5)
    kc = chunks(k)
    vc = chunks(v)
    gc = g.astype(f32).reshape(b_, n, CHUNK, h_).transpose(0, 1, 3, 2)
    bc = beta.astype(f32).reshape(b_, n, CHUNK, h_).transpose(0, 1, 3, 2)
    gcum = jnp.cumsum(gc, -1)
    causal = jnp.tril(jnp.ones((CHUNK, CHUNK), bool))
    strict = jnp.tril(jnp.ones((CHUNK, CHUNK), bool), k=-1)
    diff = gcum[..., :, None] - gcum[..., None, :]
    decay = jnp.exp(jnp.where(causal, diff, NEG_INF))
    kk = jnp.einsum('bnhid,bnhjd->bnhij', kc, kc)
    lower = jnp.eye(CHUNK, dtype=f32) + jnp.where(strict, bc[..., :, None] * kk * decay, 0.0)
    rhs = jnp.concatenate([vc * bc[..., None], kc * (bc * jnp.exp(gcum))[..., None]], -1)
    sol = lax.linalg.triangular_solve(lower, rhs, left_side=True, lower=True, unit_diagonal=True)
    u = sol[..., :dv]
    w = sol[..., dv:]
    a_qk = jnp.einsum('bnhid,bnhjd->bnhij', qc, kc) * decay
    q_dec = qc * jnp.exp(gcum)[..., None]
    k_dec = kc * jnp.exp(gcum[..., -1:] - gcum)[..., None]
    g_last = jnp.exp(gcum[..., -1])

    def step(s, inp):
        u_i, w_i, a_i, qd_i, kd_i, gl_i = inp
        v_new = u_i - jnp.einsum('bhcd,bhde->bhce', w_i, s)
        o = jnp.einsum('bhcd,bhde->bhce', qd_i, s) + jnp.einsum('bhij,bhje->bhie', a_i, v_new)
        s = s * gl_i[..., None, None] + jnp.einsum('bhcd,bhce->bhde', kd_i, v_new)
        return s, o

    xs = (u.swapaxes(0, 1), w.swapaxes(0, 1), a_qk.swapaxes(0, 1),
          q_dec.swapaxes(0, 1), k_dec.swapaxes(0, 1), g_last.swapaxes(0, 1))
    s0 = jnp.zeros((b_, h_, dk, dv), f32)
    _, o = lax.scan(step, s0, xs)
    return o.transpose(1, 0, 3, 2, 4).reshape(b_, t_, h_, dv)


def alibi_slopes(n):
    return jnp.exp2(-8.0 * jnp.arange(1, n + 1, dtype=jnp.float32) / n)


def sliding_window_sink_attention(q, k, v, sinks):
    b_, t_, _ = q.shape
    nb = t_ // BLOCK
    qb = q.reshape(b_, nb, BLOCK, SWA_KV_HEADS, SWA_GROUP, HEAD_DIM)

    def with_prev(a):
        a = a.reshape(b_, nb, BLOCK, SWA_KV_HEADS, HEAD_DIM)
        prev = jnp.pad(a, ((0, 0), (1, 0), (0, 0), (0, 0), (0, 0)))[:, :-1]
        return jnp.concatenate([prev, a], axis=2)

    kw = with_prev(k)
    vw = with_prev(v)
    s = jnp.einsum('bnqhgd,bnkhd->bnhgqk', qb, kw).astype(jnp.float32) * (HEAD_DIM ** -0.5)
    qi = jnp.arange(BLOCK)[:, None]
    kj = jnp.arange(2 * BLOCK)[None, :]
    dist = qi - kj + BLOCK
    in_window = (dist >= 0) & (dist < WINDOW)
    blk = jnp.arange(nb)[:, None, None]
    valid = in_window[None] & ((blk > 0) | (kj[None] >= BLOCK))
    slopes = alibi_slopes(SWA_Q_HEADS).reshape(SWA_KV_HEADS, SWA_GROUP)
    logits = s - slopes[:, :, None, None] * dist.astype(jnp.float32)
    logits = jnp.where(valid[None, :, None, None], logits, NEG_INF)
    sink = jnp.broadcast_to(sinks.astype(jnp.float32).reshape(SWA_KV_HEADS, SWA_GROUP)[:, :, None, None],
                            logits.shape[:-1] + (1,))
    probs = jax.nn.softmax(jnp.concatenate([logits, sink], -1), -1)[..., :-1]
    o = jnp.einsum('bnhgqk,bnkhd->bnqhgd', probs.astype(v.dtype), vw)
    return o.reshape(b_, t_, SWA_Q_W)


def hybrid_mixer(h, w_in, conv_w, a_log, dt_bias, dn_norm_w, sinks, w_o):
    b_, t_, _ = h.shape
    f32 = jnp.float32
    proj = jnp.einsum('btd,dn->btn', h, w_in)
    idx = np.cumsum(SPLITS)[:-1].tolist()
    dn_qkv, dn_z, dn_b, dn_a, sw_q, sw_k, sw_v = jnp.split(proj, idx, axis=-1)
    qkv = jax.nn.silu(causal_short_conv(dn_qkv, conv_w))
    q, k, v = jnp.split(qkv, [DN_QK_W, 2 * DN_QK_W], axis=-1)
    q = l2norm(q.reshape(b_, t_, DN_HEADS, DN_DK).astype(f32))
    k = l2norm(k.reshape(b_, t_, DN_HEADS, DN_DK).astype(f32))
    v = v.reshape(b_, t_, DN_HEADS, DN_DV)
    beta = jax.nn.sigmoid(dn_b.astype(f32))
    g = -jnp.exp(a_log.astype(f32)) * jax.nn.softplus(dn_a.astype(f32) + dt_bias.astype(f32))
    o = gated_delta_rule(q, k, v, g, beta)
    o = o * lax.rsqrt(jnp.mean(jnp.square(o), -1, keepdims=True) + 1e-6) * dn_norm_w.astype(f32)
    o = o * jax.nn.silu(dn_z.astype(f32).reshape(b_, t_, DN_HEADS, DN_DV))
    out_a = o.reshape(b_, t_, DN_V_W).astype(h.dtype)
    out_b = sliding_window_sink_attention(sw_q, sw_k, sw_v, sinks)
    return jnp.einsum('btm,md->btd', jnp.concatenate([out_a, out_b], -1), w_o)


def swiglu(x, w1, w3, w2):
    return jnp.einsum('btf,fd->btd', jax.nn.silu(jnp.einsum('btd,df->btf', x, w1)) * jnp.einsum('btd,df->btf', x, w3), w2)


def moe_ffn(x, router_w, w1, w3, w2):
    logits = jnp.einsum('btd,de->bte', x, router_w).astype(jnp.float32)
    top_v, top_i = lax.top_k(logits, TOP_K)
    top_w = jax.nn.softmax(top_v, -1)
    gates = jnp.sum(jax.nn.one_hot(top_i, N_EXPERTS, dtype=jnp.float32) * top_w[..., None], axis=-2)
    out = jnp.zeros_like(x)
    for e in range(N_EXPERTS):
        out = out + gates[..., e:e + 1].astype(x.dtype) * swiglu(x, w1[e], w3[e], w2[e])
    return out


def setup_inputs(seed: int = 0) -> dict:
    key = jax.random.key(seed)
    ks = jax.random.split(key, 24)
    f32 = jnp.float32
    nrm = lambda k, shape, scale: jax.random.normal(k, shape, f32) * scale
    dt = jnp.exp(jax.random.uniform(ks[4], (DEPTH, DN_HEADS), f32, minval=math.log(1e-3), maxval=math.log(1e-1)))
    return {
        'x': nrm(ks[0], (BATCH, SEQ, D_MODEL), 1.0),
        'w_in': nrm(ks[1], (DEPTH, D_MODEL, N_IN), D_MODEL ** -0.5),
        'conv_w': nrm(ks[2], (DEPTH, CONV_K, DN_CONV_W), CONV_K ** -0.5),
        'a_log': jnp.log(jax.random.uniform(ks[3], (DEPTH, DN_HEADS), f32, minval=1.0, maxval=16.0)),
        'dt_bias': dt + jnp.log(-jnp.expm1(-dt)),
        'dn_norm_w': 1.0 + nrm(ks[5], (DEPTH, DN_DV), 0.02),
        'sinks': nrm(ks[6], (DEPTH, SWA_Q_HEADS), 1.0),
        'w_o': nrm(ks[7], (DEPTH, MIX_WIDTH, D_MODEL), DN_BETA * MIX_WIDTH ** -0.5),
        'ln1_g': 1.0 + nrm(ks[8], (DEPTH, D_MODEL), 0.02),
        'ln1_b': nrm(ks[9], (DEPTH, D_MODEL), 0.02),
        'ffn_w1': nrm(ks[10], (N_DENSE, D_MODEL, D_FF), D_MODEL ** -0.5),
        'ffn_w3': nrm(ks[11], (N_DENSE, D_MODEL, D_FF), D_MODEL ** -0.5),
        'ffn_w2': nrm(ks[12], (N_DENSE, D_FF, D_MODEL), DN_BETA * D_FF ** -0.5),
        'router_w': nrm(ks[13], (N_MOE, D_MODEL, N_EXPERTS), D_MODEL ** -0.5),
        'exp_w1': nrm(ks[14], (N_MOE, N_EXPERTS, D_MODEL, D_FF_EXPERT), D_MODEL ** -0.5),
        'exp_w3': nrm(ks[15], (N_MOE, N_EXPERTS, D_MODEL, D_FF_EXPERT), D_MODEL ** -0.5),
        'exp_w2': nrm(ks[16], (N_MOE, N_EXPERTS, D_FF_EXPERT, D_MODEL), DN_BETA * D_FF_EXPERT ** -0.5),
        'ln2_g': 1.0 + nrm(ks[17], (DEPTH, D_MODEL), 0.02),
        'ln2_b': nrm(ks[18], (DEPTH, D_MODEL), 0.02),
    }


def reference(x, w_in, conv_w, a_log, dt_bias, dn_norm_w, sinks, w_o, ln1_g, ln1_b,
              ffn_w1, ffn_w3, ffn_w2, router_w, exp_w1, exp_w3, exp_w2, ln2_g, ln2_b):
    for i in range(DEPTH):
        mix = hybrid_mixer(x, w_in[i], conv_w[i], a_log[i], dt_bias[i], dn_norm_w[i], sinks[i], w_o[i])
        x = layer_norm(DN_ALPHA * x + mix, ln1_g[i], ln1_b[i])
        if i % 2 == 0:
            j = i // 2
            f = swiglu(x, ffn_w1[j], ffn_w3[j], ffn_w2[j])
        else:
            j = i // 2
            f = moe_ffn(x, router_w[j], exp_w1[j], exp_w3[j], exp_w2[j])
        x = layer_norm(DN_ALPHA * x + f, ln2_g[i], ln2_b[i])
    return x
```

```python
import functools

import jax
import jax.numpy as jnp
from jax import lax
from jax.experimental import pallas as pl
from jax.experimental.pallas import tpu as pltpu

F32 = jnp.float32
BF16 = jnp.bfloat16
I32 = jnp.int32

HEAD_DIM = 128
DN_HEADS = 16
DN_QK_W = DN_HEADS * HEAD_DIM
DN_V_W = DN_HEADS * HEAD_DIM
CONV_K = 4
SWA_Q_HEADS = 16
SWA_KV_HEADS = 4
SWA_GROUP = SWA_Q_HEADS // SWA_KV_HEADS
SWA_Q_W = SWA_Q_HEADS * HEAD_DIM
SWA_KV_W = SWA_KV_HEADS * HEAD_DIM
WINDOW = 128
N_EXPERTS = 8
LN_EPS = 1e-5
NEG_INF = -1e30

_IN_QKV = 2 * DN_QK_W + DN_V_W
_IN_Z = _IN_QKV + DN_V_W
_IN_B = _IN_Z + DN_HEADS
_IN_A = _IN_B + DN_HEADS
Q_OFF, K_OFF, V_OFF, Z_OFF = 0, DN_QK_W, 2 * DN_QK_W, _IN_QKV
SWQ_OFF = _IN_Z
SWK_OFF = SWQ_OFF + SWA_Q_W
SWV_OFF = SWK_OFF + SWA_KV_W
MAIN_W = SWV_OFF + SWA_KV_W

LANES = 128
SUBLANES = 8
VMEM_BYTES_V7X = 64 * 1024 * 1024
MIB = 1024 * 1024

CHUNK = 128


def _cparams(semantics, vmem_mib):
    assert vmem_mib * MIB < VMEM_BYTES_V7X
    return pltpu.CompilerParams(dimension_semantics=semantics, vmem_limit_bytes=vmem_mib * MIB)


def _pick(n, pref):
    t = min(n, pref)
    while n % t:
        t //= 2
    return t


def _mm_body(a_ref, b_ref, o_ref):
    o_ref[...] = jnp.dot(a_ref[...], b_ref[...], preferred_element_type=F32).astype(o_ref.dtype)


def _mm_acc_body(a_ref, b_ref, o_ref, acc_ref):
    k = pl.program_id(2)
    p = jnp.dot(a_ref[...], b_ref[...], preferred_element_type=F32)

    @pl.when(k == 0)
    def _():
        acc_ref[...] = p

    @pl.when(k > 0)
    def _():
        acc_ref[...] += p

    @pl.when(k == pl.num_programs(2) - 1)
    def _():
        o_ref[...] = acc_ref[...].astype(o_ref.dtype)


def matmul(a, b, out_dtype, *, bm=1024, bn=1024, bk=4096, name="mm"):
    m, k = a.shape
    k2, n = b.shape
    assert k == k2
    bm, bn, bk = _pick(m, bm), _pick(n, bn), _pick(k, bk)
    nk = k // bk
    osz = jnp.dtype(out_dtype).itemsize
    vmem = 2 * (bm * bk * 2 + bk * bn * 2 + bm * bn * osz) + bm * bn * 4 * (2 if nk > 1 else 1)
    vmem_mib = vmem // MIB + 8
    if nk == 1:
        return pl.pallas_call(
            _mm_body,
            grid=(m // bm, n // bn),
            in_specs=[pl.BlockSpec((bm, bk), lambda i, j: (i, 0)),
                      pl.BlockSpec((bk, bn), lambda i, j: (0, j))],
            out_specs=pl.BlockSpec((bm, bn), lambda i, j: (i, j)),
            out_shape=jax.ShapeDtypeStruct((m, n), out_dtype),
            compiler_params=_cparams(("parallel", "parallel"), vmem_mib),
            name=name,
        )(a, b)
    return pl.pallas_call(
        _mm_acc_body,
        grid=(m // bm, n // bn, nk),
        in_specs=[pl.BlockSpec((bm, bk), lambda i, j, kk: (i, kk)),
                  pl.BlockSpec((bk, bn), lambda i, j, kk: (kk, j))],
        out_specs=pl.BlockSpec((bm, bn), lambda i, j, kk: (i, j)),
        out_shape=jax.ShapeDtypeStruct((m, n), out_dtype),
        scratch_shapes=[pltpu.VMEM((bm, bn), F32)],
        compiler_params=_cparams(("parallel", "parallel", "arbitrary"), vmem_mib),
        name=name,
    )(a, b)


def _layer_norm_rows(r, g, b):
    mu = jnp.mean(r, -1, keepdims=True)
    c = r - mu
    var = jnp.mean(c * c, -1, keepdims=True)
    return c * lax.rsqrt(var + LN_EPS) * g + b


def _add_ln_body(x_ref, y_ref, g_ref, b_ref, of_ref, ob_ref, *, alpha):
    r = alpha * x_ref[...] + y_ref[...].astype(F32)
    o = _layer_norm_rows(r, g_ref[...], b_ref[...])
    of_ref[...] = o
    ob_ref[...] = o.astype(BF16)


def add_layer_norm(x, y, g, b, alpha, *, bm=256):
    t, d = x.shape
    bm = _pick(t, bm)
    row = pl.BlockSpec((bm, d), lambda i: (i, 0))
    vec = pl.BlockSpec((1, d), lambda i: (0, 0))
    vmem_mib = (2 * bm * d * (4 + y.dtype.itemsize + 4 + 2)) // MIB + 8
    return pl.pallas_call(
        functools.partial(_add_ln_body, alpha=alpha),
        grid=(t // bm,),
        in_specs=[row, row, vec, vec],
        out_specs=[row, row],
        out_shape=[jax.ShapeDtypeStruct((t, d), F32), jax.ShapeDtypeStruct((t, d), BF16)],
        compiler_params=_cparams(("parallel",), vmem_mib),
        name="add_ln",
    )(x, y, g.reshape(1, d), b.reshape(1, d))


def _swiglu_up_body(x_ref, w1_ref, w3_ref, h_ref):
    x = x_ref[...]
    a = jnp.dot(x, w1_ref[...], preferred_element_type=F32)
    b = jnp.dot(x, w3_ref[...], preferred_element_type=F32)
    h_ref[...] = (a * jax.nn.sigmoid(a) * b).astype(h_ref.dtype)


def swiglu_up(x, w1, w3, *, bm=1024, bn=512):
    t, d = x.shape
    f = w1.shape[1]
    bm, bn = _pick(t, bm), _pick(f, bn)
    vmem_mib = (2 * (bm * d * 2 + 2 * d * bn * 2 + bm * bn * 2) + 3 * bm * bn * 4) // MIB + 8
    return pl.pallas_call(
        _swiglu_up_body,
        grid=(t // bm, f // bn),
        in_specs=[pl.BlockSpec((bm, d), lambda i, j: (i, 0)),
                  pl.BlockSpec((d, bn), lambda i, j: (0, j)),
                  pl.BlockSpec((d, bn), lambda i, j: (0, j))],
        out_specs=pl.BlockSpec((bm, bn), lambda i, j: (i, j)),
        out_shape=jax.ShapeDtypeStruct((t, f), BF16),
        compiler_params=_cparams(("parallel", "parallel"), vmem_mib),
        name="swiglu_up",
    )(x, w1, w3)


def _softplus(x):
    return jnp.maximum(x, 0.0) + jnp.log(1.0 + jnp.exp(-jnp.abs(x)))


def _dn_gate_body(x_ref, wba_ref, wat_ref, alog_r_ref, dtb_r_ref, alog_c_ref, dtb_c_ref,
                  beta_ref, gam_ref, gamt_ref, *, tb):
    x = x_ref[...]
    p = jnp.dot(x, wba_ref[...], preferred_element_type=F32)
    beta_ref[...] = jax.nn.sigmoid(p[:, :LANES])
    g = -jnp.exp(alog_r_ref[...]) * _softplus(p[:, LANES:] + dtb_r_ref[...])
    pt = lax.dot_general(wat_ref[...], x, (((1,), (1,)), ((), ())),
                         preferred_element_type=F32)
    gt = -jnp.exp(alog_c_ref[...]) * _softplus(pt + dtb_c_ref[...])
    r = lax.broadcasted_iota(I32, (tb, tb), 0)
    c = lax.broadcasted_iota(I32, (tb, tb), 1)
    same = (r // CHUNK) == (c // CHUNK)
    lower = jnp.where(same & (r >= c), 1.0, 0.0).astype(F32)
    upper = jnp.where(same & (r <= c), 1.0, 0.0).astype(F32)
    gam_ref[...] = jnp.dot(lower, g, precision=lax.Precision.HIGHEST, preferred_element_type=F32)
    gamt_ref[...] = jnp.dot(gt, upper, precision=lax.Precision.HIGHEST, preferred_element_type=F32)


def dn_gates(x, wba, wat, a_log, dt_bias, *, tb=512):
    t, d = x.shape
    tb = _pick(t, tb)
    pad = lambda v: jnp.zeros((1, LANES), F32).at[0, :DN_HEADS].set(v.astype(F32))
    col = lambda v: v.astype(F32).reshape(DN_HEADS, 1)
    full = lambda shape: pl.BlockSpec(shape, lambda i: (0, 0))
    return pl.pallas_call(
        functools.partial(_dn_gate_body, tb=tb),
        grid=(t // tb,),
        in_specs=[pl.BlockSpec((tb, d), lambda i: (i, 0)),
                  full((d, 2 * LANES)), full((DN_HEADS, d)),
                  full((1, LANES)), full((1, LANES)), full((DN_HEADS, 1)), full((DN_HEADS, 1))],
        out_specs=[pl.BlockSpec((tb, LANES), lambda i: (i, 0)),
                   pl.BlockSpec((tb, LANES), lambda i: (i, 0)),
                   pl.BlockSpec((DN_HEADS, tb), lambda i: (0, i))],
        out_shape=[jax.ShapeDtypeStruct((t, LANES), F32),
                   jax.ShapeDtypeStruct((t, LANES), F32),
                   jax.ShapeDtypeStruct((DN_HEADS, t), F32)],
        compiler_params=_cparams(("parallel",), 32),
        name="dn_gates",
    )(x, wba, wat, pad(a_log), pad(dt_bias), col(a_log), col(dt_bias))


def _mm16(a, b):
    return jnp.dot(a.astype(BF16), b.astype(BF16), preferred_element_type=F32)


def _unit_lower_inverse(a, eye, diag_blocks):
    d = jnp.where(diag_blocks, a, 0.0)
    n = a - d
    d2 = _mm16(d, d)
    d4 = _mm16(d2, d2)
    d8 = _mm16(d4, d4)
    x = _mm16(eye - d, eye + d2)
    x = _mm16(x, eye + d4)
    x = _mm16(x, eye + d8)
    m = _mm16(x, n)
    m2 = _mm16(m, m)
    m4 = _mm16(m2, m2)
    y = _mm16(eye - m, eye + m2)
    y = _mm16(y, eye + m4)
    return _mm16(y, x)


def _delta_body(q_ref, k_ref, v_ref, z_ref, wq_ref, wk_ref, wv_ref, beta_ref, gam_ref, gamt_ref,
                nw_ref, o_ref, s_ref, halo_ref, ext_ref, act_ref, *, hb, tb):
    hg = pl.program_id(0)
    t = pl.program_id(1)
    halo = SUBLANES

    @pl.when(t == 0)
    def _():
        s_ref[...] = jnp.zeros_like(s_ref)
        halo_ref[...] = jnp.zeros_like(halo_ref)

    for idx, (ref, w_ref) in enumerate(((q_ref, wq_ref), (k_ref, wk_ref), (v_ref, wv_ref))):
        cur = ref[...].astype(F32)
        ext_ref[0:halo, :] = halo_ref[idx]
        ext_ref[halo:halo + tb, :] = cur
        halo_ref[idx] = cur[tb - halo:, :]
        w = w_ref[...]
        acc = cur * w[CONV_K - 1:CONV_K, :]
        for tap in range(CONV_K - 1):
            acc = acc + ext_ref[pl.ds(halo - (CONV_K - 1) + tap, tb), :] * w[tap:tap + 1, :]
        act_ref[idx] = acc * jax.nn.sigmoid(acc)

    row = lax.broadcasted_iota(I32, (CHUNK, CHUNK), 0)
    col = lax.broadcasted_iota(I32, (CHUNK, CHUNK), 1)
    causal = row >= col
    strict = row > col
    diag_blocks = (row // 16) == (col // 16)
    eye = jnp.where(row == col, 1.0, 0.0).astype(F32)
    lane = lax.broadcasted_iota(I32, (tb, LANES), 1)
    scale = HEAD_DIM ** -0.5
    nw = nw_ref[...]

    for hl in range(hb):
        head = hg * hb + hl
        sel = lane == head
        bcol = jnp.sum(jnp.where(sel, beta_ref[...], 0.0), axis=-1, keepdims=True)
        gcol = jnp.sum(jnp.where(sel, gam_ref[...], 0.0), axis=-1, keepdims=True)
        grow = gamt_ref[pl.ds(head, 1), :]
        cs = slice(hl * HEAD_DIM, (hl + 1) * HEAD_DIM)
        for c in range(tb // CHUNK):
            rs = slice(c * CHUNK, (c + 1) * CHUNK)
            q = act_ref[0, rs, cs]
            k = act_ref[1, rs, cs]
            v = act_ref[2, rs, cs]
            q = q * lax.rsqrt(jnp.sum(q * q, -1, keepdims=True) + 1e-6) * scale
            k = k * lax.rsqrt(jnp.sum(k * k, -1, keepdims=True) + 1e-6)
            bc = jnp.broadcast_to(bcol[rs, :], (CHUNK, CHUNK))
            gc = jnp.broadcast_to(gcol[rs, :], (CHUNK, CHUNK))
            gr = jnp.broadcast_to(grow[:, rs], (CHUNK, CHUNK))
            g_last = jnp.broadcast_to(grow[:, (c + 1) * CHUNK - 1:(c + 1) * CHUNK], (CHUNK, CHUNK))
            decay = jnp.exp(jnp.where(causal, gc - gr, NEG_INF))
            egc = jnp.exp(gc)
            kt = k.T
            gram = _mm16(jnp.concatenate([q, k], axis=0), kt)
            a_qk = gram[:CHUNK] * decay
            a_kk = jnp.where(strict, bc * gram[CHUNK:] * decay, 0.0)
            tinv = _unit_lower_inverse(a_kk, eye, diag_blocks)
            sol = _mm16(tinv, jnp.concatenate([v * bc, k * (bc * egc)], axis=1))
            u = sol[:, :HEAD_DIM]
            w = sol[:, HEAD_DIM:]
            s = s_ref[hl]
            ws_qs = _mm16(jnp.concatenate([w, q * egc], axis=0), s)
            v_new = u - ws_qs[:CHUNK]
            kdt = kt * jnp.exp(g_last - gr)
            av_ds = _mm16(jnp.concatenate([a_qk, kdt], axis=0), v_new)
            o = ws_qs[CHUNK:] + av_ds[:CHUNK]
            s_ref[hl] = s * jnp.exp(g_last) + av_ds[CHUNK:]
            o = o * lax.rsqrt(jnp.mean(o * o, -1, keepdims=True) + 1e-6) * nw
            zz = z_ref[rs, cs].astype(F32)
            o_ref[rs, cs] = (o * (zz * jax.nn.sigmoid(zz))).astype(o_ref.dtype)


def delta_net(proj, conv_w, beta, gam, gamt, norm_w, *, hb=2, tb=512):
    t = proj.shape[0]
    tb = _pick(t, tb)
    w = hb * HEAD_DIM
    nq, nk, nv, nz = Q_OFF // w, K_OFF // w, V_OFF // w, Z_OFF // w
    pspec = lambda off: pl.BlockSpec((tb, w), lambda h, i: (i, off + h))
    cspec = lambda off: pl.BlockSpec((CONV_K, w), lambda h, i: (0, off + h))
    return pl.pallas_call(
        functools.partial(_delta_body, hb=hb, tb=tb),
        grid=(DN_HEADS // hb, t // tb),
        in_specs=[pspec(nq), pspec(nk), pspec(nv), pspec(nz),
                  cspec(nq), cspec(nk), cspec(nv),
                  pl.BlockSpec((tb, LANES), lambda h, i: (i, 0)),
                  pl.BlockSpec((tb, LANES), lambda h, i: (i, 0)),
                  pl.BlockSpec((DN_HEADS, tb), lambda h, i: (0, i)),
                  pl.BlockSpec((1, HEAD_DIM), lambda h, i: (0, 0))],
        out_specs=pl.BlockSpec((tb, w), lambda h, i: (i, h)),
        out_shape=jax.ShapeDtypeStruct((t, DN_V_W), BF16),
        scratch_shapes=[pltpu.VMEM((hb, HEAD_DIM, HEAD_DIM), F32),
                        pltpu.VMEM((3, SUBLANES, w), F32),
                        pltpu.VMEM((SUBLANES + tb, w), F32),
                        pltpu.VMEM((3, tb, w), F32)],
        compiler_params=_cparams(("parallel", "arbitrary"), 40),
        name="delta_net",
    )(proj, proj, proj, proj, conv_w, conv_w, conv_w, beta, gam, gamt,
      norm_w.astype(F32).reshape(1, HEAD_DIM))


def _swa_body(sink_ref, q_ref, kc_ref, kp_ref, vc_ref, vp_ref, o_ref):
    i = pl.program_id(0)
    qi = lax.broadcasted_iota(I32, (WINDOW, 2 * WINDOW), 0)
    kj = lax.broadcasted_iota(I32, (WINDOW, 2 * WINDOW), 1)
    dist = qi - kj + WINDOW
    first_key = jnp.where(i > 0, 0, WINDOW)
    valid = (dist >= 0) & (dist < WINDOW) & (kj >= first_key)
    distf = dist.astype(F32)
    scale = HEAD_DIM ** -0.5
    for h in range(SWA_KV_HEADS):
        hs = slice(h * HEAD_DIM, (h + 1) * HEAD_DIM)
        kk = jnp.concatenate([kp_ref[:, hs], kc_ref[:, hs]], axis=0)
        vv = jnp.concatenate([vp_ref[:, hs], vc_ref[:, hs]], axis=0)
        for g in range(SWA_GROUP):
            hq = h * SWA_GROUP + g
            slope = 2.0 ** (-8.0 * (hq + 1) / SWA_Q_HEADS)
            qs = slice(hq * HEAD_DIM, (hq + 1) * HEAD_DIM)
            s = lax.dot_general(q_ref[:, qs], kk, (((1,), (1,)), ((), ())),
                                preferred_element_type=F32) * scale
            logits = jnp.where(valid, s - slope * distf, NEG_INF)
            sink = sink_ref[hq]
            m = jnp.maximum(jnp.max(logits, -1, keepdims=True), sink)
            e = jnp.exp(logits - m)
            denom = jnp.sum(e, -1, keepdims=True) + jnp.exp(sink - m)
            p = (e / denom).astype(BF16)
            o_ref[:, qs] = jnp.dot(p, vv, preferred_element_type=F32).astype(o_ref.dtype)


def swa_attention(proj, sinks):
    t = proj.shape[0]
    nb = t // WINDOW
    qb, kb, vb = SWQ_OFF // SWA_Q_W, SWK_OFF // SWA_KV_W, SWV_OFF // SWA_KV_W
    cur = lambda blk: pl.BlockSpec((WINDOW, SWA_KV_W), lambda i: (i, blk))
    prev = lambda blk: pl.BlockSpec((WINDOW, SWA_KV_W), lambda i: (jnp.maximum(i - 1, 0), blk))
    return pl.pallas_call(
        _swa_body,
        grid=(nb,),
        in_specs=[pl.BlockSpec(memory_space=pltpu.SMEM),
                  pl.BlockSpec((WINDOW, SWA_Q_W), lambda i: (i, qb)),
                  cur(kb), prev(kb), cur(vb), prev(vb)],
        out_specs=pl.BlockSpec((WINDOW, SWA_Q_W), lambda i: (i, 0)),
        out_shape=jax.ShapeDtypeStruct((t, SWA_Q_W), BF16),
        compiler_params=_cparams(("parallel",), 32),
        name="swa",
    )(sinks.astype(F32), proj, proj, proj, proj, proj)


def _router_body(x_ref, w_ref, idx_ref, gate_ref):
    logits = jnp.dot(x_ref[...], w_ref[...], precision=lax.Precision.HIGHEST,
                     preferred_element_type=F32)
    lane = lax.broadcasted_iota(I32, logits.shape, 1)
    neg = jnp.float32(-jnp.inf)
    l1 = jnp.where(lane < N_EXPERTS, logits, neg)
    m1 = jnp.max(l1, -1, keepdims=True)
    i1 = jnp.min(jnp.where(l1 == m1, lane, LANES), -1, keepdims=True)
    l2 = jnp.where(lane == i1, neg, l1)
    m2 = jnp.max(l2, -1, keepdims=True)
    i2 = jnp.min(jnp.where(l2 == m2, lane, LANES), -1, keepdims=True)
    e = jnp.exp(m2 - m1)
    w1 = 1.0 / (1.0 + e)
    w2 = e * w1
    idx_ref[...] = jnp.where(lane == 0, i1, jnp.where(lane == 1, i2, 0))
    gate_ref[...] = jnp.where(lane == 0, w1, jnp.where(lane == 1, w2, 0.0))


def moe_router(x, router_w, *, bm=256):
    t, d = x.shape
    bm = _pick(t, bm)
    w = jnp.zeros((d, LANES), F32).at[:, :N_EXPERTS].set(router_w.astype(F32))
    return pl.pallas_call(
        _router_body,
        grid=(t // bm,),
        in_specs=[pl.BlockSpec((bm, d), lambda i: (i, 0)),
                  pl.BlockSpec((d, LANES), lambda i: (0, 0))],
        out_specs=[pl.BlockSpec((bm, LANES), lambda i: (i, 0)),
                   pl.BlockSpec((bm, LANES), lambda i: (i, 0))],
        out_shape=[jax.ShapeDtypeStruct((t, LANES), I32), jax.ShapeDtypeStruct((t, LANES), F32)],
        compiler_params=_cparams(("parallel",), 32),
        name="moe_router",
    )(x, w)


def _route_plan(top_i, tm):
    t = top_i.shape[0]
    n_assign = 2 * t
    e_flat = top_i.reshape(-1)
    onehot = (e_flat[:, None] == jnp.arange(N_EXPERTS, dtype=I32)[None, :]).astype(I32)
    csum = jnp.cumsum(onehot, axis=0)
    rank = jnp.sum(csum * onehot, axis=1) - 1
    counts = csum[-1]
    padded = ((counts + tm - 1) // tm) * tm
    ends = jnp.cumsum(padded)
    offs = ends - padded
    pos = offs[e_flat] + rank
    n_tiles = n_assign // tm + N_EXPERTS
    row_src = jnp.zeros((n_tiles * tm,), I32).at[pos].set(jnp.arange(n_assign, dtype=I32) // 2)
    tile_start = jnp.arange(n_tiles, dtype=I32) * tm
    tile_e = jnp.sum((tile_start[:, None] >= ends[None, :]).astype(I32), axis=1)
    tile_valid = (tile_start < ends[-1]).astype(I32)
    last_e = jnp.max(jnp.where(padded > 0, jnp.arange(N_EXPERTS, dtype=I32), 0))
    tile_e = jnp.minimum(tile_e, last_e)
    return row_src, pos[0::2], pos[1::2], tile_e, tile_valid, n_tiles


def _gather_rows_body(src_ref, x_hbm, o_ref, buf, sem, *, rows):
    base = pl.program_id(0) * rows

    def row_copy(r, src_row):
        return pltpu.make_async_copy(x_hbm.at[pl.ds(src_row, 1), :], buf.at[pl.ds(r, 1), :], sem)

    def issue(r, carry):
        row_copy(r, src_ref[base + r]).start()
        return carry

    lax.fori_loop(0, rows, issue, 0)

    def drain(r, carry):
        row_copy(r, 0).wait()
        return carry

    lax.fori_loop(0, rows, drain, 0)
    o_ref[...] = buf[...].astype(o_ref.dtype)


def gather_rows(x, row_src, *, rows=256):
    d = x.shape[1]
    p = row_src.shape[0]
    rows = _pick(p, rows)
    return pl.pallas_call(
        functools.partial(_gather_rows_body, rows=rows),
        grid_spec=pltpu.PrefetchScalarGridSpec(
            num_scalar_prefetch=1,
            grid=(p // rows,),
            in_specs=[pl.BlockSpec(memory_space=pl.ANY)],
            out_specs=pl.BlockSpec((rows, d), lambda i, src: (i, 0)),
            scratch_shapes=[pltpu.VMEM((rows, d), F32), pltpu.SemaphoreType.DMA(())]),
        out_shape=jax.ShapeDtypeStruct((p, d), BF16),
        compiler_params=_cparams(("arbitrary",), (rows * d * (4 + 2 * 2)) // MIB + 8),
        name="moe_gather",
    )(row_src, x)


def _expert_up_body(te_ref, tv_ref, x_ref, w1_ref, w3_ref, h_ref):
    i = pl.program_id(1)

    @pl.when(tv_ref[i] == 1)
    def _():
        x = x_ref[...]
        a = jnp.dot(x, w1_ref[...], preferred_element_type=F32)
        b = jnp.dot(x, w3_ref[...], preferred_element_type=F32)
        h_ref[...] = (a * jax.nn.sigmoid(a) * b).astype(h_ref.dtype)

    @pl.when(tv_ref[i] == 0)
    def _():
        h_ref[...] = jnp.zeros_like(h_ref)


def expert_up(xs, w1, w3, tile_e, tile_valid, tm, *, bn=1024):
    p, d = xs.shape
    f = w1.shape[2]
    bn = _pick(f, bn)
    wspec = pl.BlockSpec((None, d, bn), lambda j, i, te, tv: (te[i], 0, j))
    vmem_mib = (2 * (tm * d * 2 + 2 * d * bn * 2 + tm * bn * 2) + 3 * tm * bn * 4) // MIB + 8
    return pl.pallas_call(
        _expert_up_body,
        grid_spec=pltpu.PrefetchScalarGridSpec(
            num_scalar_prefetch=2,
            grid=(f // bn, p // tm),
            in_specs=[pl.BlockSpec((tm, d), lambda j, i, te, tv: (i, 0)), wspec, wspec],
            out_specs=pl.BlockSpec((tm, bn), lambda j, i, te, tv: (i, j))),
        out_shape=jax.ShapeDtypeStruct((p, f), BF16),
        compiler_params=_cparams(("parallel", "arbitrary"), vmem_mib),
        name="expert_up",
    )(tile_e, tile_valid, xs, w1, w3)


def _expert_down_body(te_ref, tv_ref, h_ref, w2_ref, y_ref):
    i = pl.program_id(1)

    @pl.when(tv_ref[i] == 1)
    def _():
        y_ref[...] = jnp.dot(h_ref[...], w2_ref[...], preferred_element_type=F32).astype(y_ref.dtype)

    @pl.when(tv_ref[i] == 0)
    def _():
        y_ref[...] = jnp.zeros_like(y_ref)


def expert_down(h, w2, tile_e, tile_valid, tm, *, bn=1024):
    p, f = h.shape
    d = w2.shape[2]
    bn = _pick(d, bn)
    vmem_mib = (2 * (tm * f * 2 + f * bn * 2 + tm * bn * 4)) // MIB + 8
    return pl.pallas_call(
        _expert_down_body,
        grid_spec=pltpu.PrefetchScalarGridSpec(
            num_scalar_prefetch=2,
            grid=(d // bn, p // tm),
            in_specs=[pl.BlockSpec((tm, f), lambda j, i, te, tv: (i, 0)),
                      pl.BlockSpec((None, f, bn), lambda j, i, te, tv: (te[i], 0, j))],
            out_specs=pl.BlockSpec((tm, bn), lambda j, i, te, tv: (i, j))),
        out_shape=jax.ShapeDtypeStruct((p, d), F32),
        compiler_params=_cparams(("parallel", "arbitrary"), vmem_mib),
        name="expert_down",
    )(tile_e, tile_valid, h, w2)


def _combine_ln_body(p1_ref, p2_ref, y_hbm, x_ref, gate_ref, g_ref, b_ref, of_ref, ob_ref,
                     buf, sem, *, rows, alpha):
    base = pl.program_id(0) * rows

    def row_copy(slot, r, src_row):
        return pltpu.make_async_copy(y_hbm.at[pl.ds(src_row, 1), :], buf.at[slot, pl.ds(r, 1), :], sem)

    def issue(r, carry):
        row_copy(0, r, p1_ref[base + r]).start()
        row_copy(1, r, p2_ref[base + r]).start()
        return carry

    lax.fori_loop(0, rows, issue, 0)

    def drain(r, carry):
        row_copy(0, r, 0).wait()
        row_copy(1, r, 0).wait()
        return carry

    lax.fori_loop(0, rows, drain, 0)
    gates = gate_ref[...]
    f = gates[:, 0:1] * buf[0] + gates[:, 1:2] * buf[1]
    o = _layer_norm_rows(alpha * x_ref[...] + f, g_ref[...], b_ref[...])
    of_ref[...] = o
    ob_ref[...] = o.astype(BF16)


def moe_combine_ln(y, pos1, pos2, gates, x, g, b, alpha, *, rows=256):
    t, d = x.shape
    rows = _pick(t, rows)
    row = lambda width: pl.BlockSpec((rows, width), lambda i, p1, p2: (i, 0))
    vec = pl.BlockSpec((1, d), lambda i, p1, p2: (0, 0))
    return pl.pallas_call(
        functools.partial(_combine_ln_body, rows=rows, alpha=alpha),
        grid_spec=pltpu.PrefetchScalarGridSpec(
            num_scalar_prefetch=2,
            grid=(t // rows,),
            in_specs=[pl.BlockSpec(memory_space=pl.ANY), row(d), row(LANES), vec, vec],
            out_specs=[row(d), row(d)],
            scratch_shapes=[pltpu.VMEM((2, rows, d), F32), pltpu.SemaphoreType.DMA(())]),
        out_shape=[jax.ShapeDtypeStruct((t, d), F32), jax.ShapeDtypeStruct((t, d), BF16)],
        compiler_params=_cparams(("arbitrary",), (rows * d * (8 + 2 * (4 + 4 + 2))) // MIB + 8),
        name="moe_combine_ln",
    )(pos1, pos2, y, x, gates, g.reshape(1, d), b.reshape(1, d))


def moe_ffn_ln(x_f32, router_w, w1, w3, w2, g, b, alpha, *, tm=512):
    idx, gates = moe_router(x_f32, router_w)
    row_src, pos1, pos2, tile_e, tile_valid, _ = _route_plan(idx[:, :2], tm)
    xs = gather_rows(x_f32, row_src)
    h = expert_up(xs, w1, w3, tile_e, tile_valid, tm)
    y = expert_down(h, w2, tile_e, tile_valid, tm)
    return moe_combine_ln(y, pos1, pos2, gates, x_f32, g, b, alpha)


def _split_w_in(w_in):
    d = w_in.shape[0]
    w_main = jnp.concatenate([w_in[:, :_IN_Z], w_in[:, _IN_A:]], axis=1).astype(BF16)
    w_ba = jnp.zeros((d, 2 * LANES), BF16)
    w_ba = w_ba.at[:, :DN_HEADS].set(w_in[:, _IN_Z:_IN_B].astype(BF16))
    w_ba = w_ba.at[:, LANES:LANES + DN_HEADS].set(w_in[:, _IN_B:_IN_A].astype(BF16))
    w_at = w_in[:, _IN_B:_IN_A].T.astype(BF16)
    return w_main, w_ba, w_at


def kernel(x, w_in, conv_w, a_log, dt_bias, dn_norm_w, sinks, w_o, ln1_g, ln1_b, ffn_w1, ffn_w3,
           ffn_w2, router_w, exp_w1, exp_w3, exp_w2, ln2_g, ln2_b):
    bsz, seq, d = x.shape
    depth = w_in.shape[0]
    alpha = (2 * depth) ** 0.25
    outs = []
    for bi in range(bsz):
        xf = x[bi].astype(F32)
        xb = xf.astype(BF16)
        for i in range(depth):
            w_main, w_ba, w_at = _split_w_in(w_in[i])
            proj = matmul(xb, w_main, BF16, name="in_proj")
            beta, gam, gamt = dn_gates(xb, w_ba, w_at, a_log[i], dt_bias[i])
            out_a = delta_net(proj, conv_w[i].astype(F32), beta, gam, gamt, dn_norm_w[i])
            out_b = swa_attention(proj, sinks[i])
            mix = matmul(jnp.concatenate([out_a, out_b], axis=1), w_o[i].astype(BF16), F32,
                         name="out_proj")
            xf, xb = add_layer_norm(xf, mix, ln1_g[i], ln1_b[i], alpha)
            j = i // 2
            if i % 2 == 0:
                h = swiglu_up(xb, ffn_w1[j].astype(BF16), ffn_w3[j].astype(BF16))
                f = matmul(h, ffn_w2[j].astype(BF16), F32, bk=2048, name="ffn_down")
                xf, xb = add_layer_norm(xf, f, ln2_g[i], ln2_b[i], alpha)
            else:
                xf, xb = moe_ffn_ln(xf, router_w[j], exp_w1[j].astype(BF16), exp_w3[j].astype(BF16),
                                    exp_w2[j].astype(BF16), ln2_g[i], ln2_b[i], alpha)
        outs.append(xf)
    return jnp.stack(outs, axis=0).astype(x.dtype)
```

```python
import functools

import jax
import jax.numpy as jnp
from jax import lax
from jax.experimental import pallas as pl
from jax.experimental.pallas import tpu as pltpu

F32 = jnp.float32
BF16 = jnp.bfloat16
I32 = jnp.int32

HEAD_DIM = 128
DN_HEADS = 16
DN_QK_W = DN_HEADS * HEAD_DIM
DN_V_W = DN_HEADS * HEAD_DIM
CONV_K = 4
SWA_Q_HEADS = 16
SWA_KV_HEADS = 4
SWA_GROUP = SWA_Q_HEADS // SWA_KV_HEADS
SWA_Q_W = SWA_Q_HEADS * HEAD_DIM
SWA_KV_W = SWA_KV_HEADS * HEAD_DIM
WINDOW = 128
N_EXPERTS = 8
LN_EPS = 1e-5
NEG_INF = -1e30

_IN_QKV = 2 * DN_QK_W + DN_V_W
_IN_Z = _IN_QKV + DN_V_W
_IN_B = _IN_Z + DN_HEADS
_IN_A = _IN_B + DN_HEADS
Q_OFF, K_OFF, V_OFF, Z_OFF = 0, DN_QK_W, 2 * DN_QK_W, _IN_QKV
SW_W = SWA_Q_W + 2 * SWA_KV_W

LANES = 128
SUBLANES = 8
VMEM_BYTES_V7X = 64 * 1024 * 1024
MIB = 1024 * 1024

CHUNK = 128


def _cparams(semantics, vmem_mib):
    assert vmem_mib * MIB < VMEM_BYTES_V7X, vmem_mib
    return pltpu.CompilerParams(dimension_semantics=semantics, vmem_limit_bytes=vmem_mib * MIB)


def _pick(n, pref):
    t = min(n, pref)
    while n % t:
        t //= 2
    return t


def _resident(block_shape, index_map):
    return pl.BlockSpec(block_shape, index_map, pipeline_mode=pl.Buffered(1))


def _mm_ws_body(*refs, n_parts, part_k):
    a_refs = refs[:n_parts]
    w_ref, o_ref, wb_ref = refs[n_parts:]

    @pl.when(pl.program_id(1) == 0)
    def _():
        wb_ref[...] = w_ref[...].astype(BF16)

    acc = jnp.dot(a_refs[0][...], wb_ref[0:part_k, :], preferred_element_type=F32)
    for p in range(1, n_parts):
        acc = acc + jnp.dot(a_refs[p][...], wb_ref[p * part_k:(p + 1) * part_k, :],
                            preferred_element_type=F32)
    o_ref[...] = acc.astype(o_ref.dtype)


def matmul_ws(a_parts, w, layer, out_dtype, *, n_cols=None, bm=512, bn=1024, name="mm"):
    m, part_k = a_parts[0].shape
    n_parts = len(a_parts)
    k = part_k * n_parts
    assert w.shape[1] == k
    n_cols = w.shape[2] if n_cols is None else n_cols
    bm, bn = _pick(m, bm), _pick(n_cols, bn)
    osz = jnp.dtype(out_dtype).itemsize
    vmem = k * bn * (4 + 2) + 2 * (bm * k * 2 + bm * bn * osz) + 2 * bm * bn * 4
    return pl.pallas_call(
        functools.partial(_mm_ws_body, n_parts=n_parts, part_k=part_k),
        grid=(n_cols // bn, m // bm),
        in_specs=[pl.BlockSpec((bm, part_k), lambda j, i: (i, 0)) for _ in range(n_parts)]
        + [_resident((None, k, bn), lambda j, i: (layer, 0, j))],
        out_specs=pl.BlockSpec((bm, bn), lambda j, i: (i, j)),
        out_shape=jax.ShapeDtypeStruct((m, n_cols), out_dtype),
        scratch_shapes=[pltpu.VMEM((k, bn), BF16)],
        compiler_params=_cparams(("arbitrary", "arbitrary"), vmem // MIB + 6),
        name=name,
    )(*a_parts, w)


def _layer_norm_rows(r, g, b):
    mu = jnp.mean(r, -1, keepdims=True)
    c = r - mu
    var = jnp.mean(c * c, -1, keepdims=True)
    return c * lax.rsqrt(var + LN_EPS) * g + b


def _add_ln_body(x_ref, y_ref, g_ref, b_ref, of_ref, ob_ref, *, alpha):
    r = alpha * x_ref[...] + y_ref[...].astype(F32)
    o = _layer_norm_rows(r, g_ref[...], b_ref[...])
    of_ref[...] = o
    ob_ref[...] = o.astype(BF16)


def add_layer_norm(x, y, g, b, alpha, *, bm=256):
    t, d = x.shape
    bm = _pick(t, bm)
    row = pl.BlockSpec((bm, d), lambda i: (i, 0))
    vec = pl.BlockSpec((1, d), lambda i: (0, 0))
    vmem_mib = (2 * bm * d * (4 + y.dtype.itemsize + 4 + 2)) // MIB + 8
    return pl.pallas_call(
        functools.partial(_add_ln_body, alpha=alpha),
        grid=(t // bm,),
        in_specs=[row, row, vec, vec],
        out_specs=[row, row],
        out_shape=[jax.ShapeDtypeStruct((t, d), F32), jax.ShapeDtypeStruct((t, d), BF16)],
        compiler_params=_cparams(("parallel",), vmem_mib),
        name="add_ln",
    )(x, y, g.reshape(1, d), b.reshape(1, d))


def _swiglu_up_body(x_ref, w1_ref, w3_ref, h_ref, w1b_ref, w3b_ref):
    @pl.when(pl.program_id(1) == 0)
    def _():
        w1b_ref[...] = w1_ref[...].astype(BF16)
        w3b_ref[...] = w3_ref[...].astype(BF16)

    x = x_ref[...]
    a = jnp.dot(x, w1b_ref[...], preferred_element_type=F32)
    b = jnp.dot(x, w3b_ref[...], preferred_element_type=F32)
    h_ref[...] = (a * jax.nn.sigmoid(a) * b).astype(h_ref.dtype)


def swiglu_up(x, w1, w3, layer, *, bm=512, bn=512):
    t, d = x.shape
    f = w1.shape[2]
    bm, bn = _pick(t, bm), _pick(f, bn)
    wspec = _resident((None, d, bn), lambda j, i: (layer, 0, j))
    vmem = 2 * d * bn * (4 + 2) + 2 * (bm * d * 2 + bm * bn * 2) + 4 * bm * bn * 4
    return pl.pallas_call(
        _swiglu_up_body,
        grid=(f // bn, t // bm),
        in_specs=[pl.BlockSpec((bm, d), lambda j, i: (i, 0)), wspec, wspec],
        out_specs=pl.BlockSpec((bm, bn), lambda j, i: (i, j)),
        out_shape=jax.ShapeDtypeStruct((t, f), BF16),
        scratch_shapes=[pltpu.VMEM((d, bn), BF16), pltpu.VMEM((d, bn), BF16)],
        compiler_params=_cparams(("arbitrary", "arbitrary"), vmem // MIB + 6),
        name="swiglu_up",
    )(x, w1, w3)


def _softplus(x):
    return jnp.maximum(x, 0.0) + jnp.log(1.0 + jnp.exp(-jnp.abs(x)))


def _dn_gate_body(x_ref, wba_ref, wat_ref, alog_r_ref, dtb_r_ref, alog_c_ref, dtb_c_ref,
                  beta_ref, gam_ref, gamt_ref, *, tb):
    x = x_ref[...]
    p = jnp.dot(x, wba_ref[...], preferred_element_type=F32)
    beta_ref[...] = jax.nn.sigmoid(p[:, :LANES])
    g = -jnp.exp(alog_r_ref[...]) * _softplus(p[:, LANES:] + dtb_r_ref[...])
    pt = lax.dot_general(wat_ref[...], x, (((1,), (1,)), ((), ())),
                         preferred_element_type=F32)
    gt = -jnp.exp(alog_c_ref[...]) * _softplus(pt + dtb_c_ref[...])
    r = lax.broadcasted_iota(I32, (tb, tb), 0)
    c = lax.broadcasted_iota(I32, (tb, tb), 1)
    same = (r // CHUNK) == (c // CHUNK)
    lower = jnp.where(same & (r >= c), 1.0, 0.0).astype(F32)
    upper = jnp.where(same & (r <= c), 1.0, 0.0).astype(F32)
    gam_ref[...] = jnp.dot(lower, g, precision=lax.Precision.HIGHEST, preferred_element_type=F32)
    gamt_ref[...] = jnp.dot(gt, upper, precision=lax.Precision.HIGHEST, preferred_element_type=F32)


def dn_gates(x, wba, wat, a_log, dt_bias, *, tb=512):
    t, d = x.shape
    tb = _pick(t, tb)
    pad = lambda v: jnp.zeros((1, LANES), F32).at[0, :DN_HEADS].set(v.astype(F32))
    col = lambda v: v.astype(F32).reshape(DN_HEADS, 1)
    full = lambda shape: pl.BlockSpec(shape, lambda i: (0, 0))
    return pl.pallas_call(
        functools.partial(_dn_gate_body, tb=tb),
        grid=(t // tb,),
        in_specs=[pl.BlockSpec((tb, d), lambda i: (i, 0)),
                  full((d, 2 * LANES)), full((DN_HEADS, d)),
                  full((1, LANES)), full((1, LANES)), full((DN_HEADS, 1)), full((DN_HEADS, 1))],
        out_specs=[pl.BlockSpec((tb, LANES), lambda i: (i, 0)),
                   pl.BlockSpec((tb, LANES), lambda i: (i, 0)),
                   pl.BlockSpec((DN_HEADS, tb), lambda i: (0, i))],
        out_shape=[jax.ShapeDtypeStruct((t, LANES), F32),
                   jax.ShapeDtypeStruct((t, LANES), F32),
                   jax.ShapeDtypeStruct((DN_HEADS, t), F32)],
        compiler_params=_cparams(("parallel",), 32),
        name="dn_gates",
    )(x, wba, wat, pad(a_log), pad(dt_bias), col(a_log), col(dt_bias))


def _mm16(a, b):
    return jnp.dot(a.astype(BF16), b.astype(BF16), preferred_element_type=F32)


def _mm16_each(xs, ys):
    return [_mm16(x, y) for x, y in zip(xs, ys)]


def _unit_lower_inverse_each(a_list, eye, diag_blocks):
    d = [jnp.where(diag_blocks, a, 0.0) for a in a_list]
    n = [a - di for a, di in zip(a_list, d)]
    d2 = _mm16_each(d, d)
    x = _mm16_each([eye - t for t in d], [eye + t for t in d2])
    d4 = _mm16_each(d2, d2)
    x = _mm16_each(x, [eye + t for t in d4])
    d8 = _mm16_each(d4, d4)
    x = _mm16_each(x, [eye + t for t in d8])
    m = _mm16_each(x, n)
    m2 = _mm16_each(m, m)
    y = _mm16_each([eye - t for t in m], [eye + t for t in m2])
    m4 = _mm16_each(m2, m2)
    y = _mm16_each(y, [eye + t for t in m4])
    return _mm16_each(y, x)


def _delta_body(q_ref, k_ref, v_ref, z_ref, wq_ref, wk_ref, wv_ref, beta_ref, gam_ref, gamt_ref,
                nw_ref, o_ref, s_ref, halo_ref, ext_ref, act_ref, *, hb, tb):
    hg = pl.program_id(0)
    t = pl.program_id(1)
    halo = SUBLANES

    @pl.when(t == 0)
    def _():
        s_ref[...] = jnp.zeros_like(s_ref)
        halo_ref[...] = jnp.zeros_like(halo_ref)

    for idx, (ref, w_ref) in enumerate(((q_ref, wq_ref), (k_ref, wk_ref), (v_ref, wv_ref))):
        cur = ref[...].astype(F32)
        ext_ref[0:halo, :] = halo_ref[idx]
        ext_ref[halo:halo + tb, :] = cur
        halo_ref[idx] = cur[tb - halo:, :]
        w = w_ref[...]
        acc = cur * w[CONV_K - 1:CONV_K, :]
        for tap in range(CONV_K - 1):
            acc = acc + ext_ref[pl.ds(halo - (CONV_K - 1) + tap, tb), :] * w[tap:tap + 1, :]
        act_ref[idx] = acc * jax.nn.sigmoid(acc)

    row = lax.broadcasted_iota(I32, (CHUNK, CHUNK), 0)
    col = lax.broadcasted_iota(I32, (CHUNK, CHUNK), 1)
    causal = row >= col
    strict = row > col
    diag_blocks = (row // 16) == (col // 16)
    eye = jnp.where(row == col, 1.0, 0.0).astype(F32)
    lane = lax.broadcasted_iota(I32, (tb, LANES), 1)
    scale = HEAD_DIM ** -0.5
    nw = nw_ref[...]
    heads = range(hb)
    sq = (CHUNK, CHUNK)

    bcol, gcol, grow = [], [], []
    for hl in heads:
        head = hg * hb + hl
        sel = lane == head
        bcol.append(jnp.sum(jnp.where(sel, beta_ref[...], 0.0), axis=-1, keepdims=True))
        gcol.append(jnp.sum(jnp.where(sel, gam_ref[...], 0.0), axis=-1, keepdims=True))
        grow.append(gamt_ref[pl.ds(head, 1), :])

    for c in range(tb // CHUNK):
        rs = slice(c * CHUNK, (c + 1) * CHUNK)
        cs = [slice(hl * HEAD_DIM, (hl + 1) * HEAD_DIM) for hl in heads]
        q = [act_ref[0, rs, cs[h]] for h in heads]
        k = [act_ref[1, rs, cs[h]] for h in heads]
        v = [act_ref[2, rs, cs[h]] for h in heads]
        q = [x * lax.rsqrt(jnp.sum(x * x, -1, keepdims=True) + 1e-6) * scale for x in q]
        k = [x * lax.rsqrt(jnp.sum(x * x, -1, keepdims=True) + 1e-6) for x in k]
        bc = [jnp.broadcast_to(bcol[h][rs, :], sq) for h in heads]
        gc = [jnp.broadcast_to(gcol[h][rs, :], sq) for h in heads]
        gr = [jnp.broadcast_to(grow[h][:, rs], sq) for h in heads]
        g_last = [jnp.broadcast_to(grow[h][:, (c + 1) * CHUNK - 1:(c + 1) * CHUNK], sq) for h in heads]
        decay = [jnp.exp(jnp.where(causal, gc[h] - gr[h], NEG_INF)) for h in heads]
        egc = [jnp.exp(gc[h]) for h in heads]
        kt = [k[h].T for h in heads]
        gram = _mm16_each([jnp.concatenate([q[h], k[h]], axis=0) for h in heads], kt)
        a_qk = [gram[h][:CHUNK] * decay[h] for h in heads]
        a_kk = [jnp.where(strict, bc[h] * gram[h][CHUNK:] * decay[h], 0.0) for h in heads]
        tinv = _unit_lower_inverse_each(a_kk, eye, diag_blocks)
        sol = _mm16_each(tinv, [jnp.concatenate([v[h] * bc[h], k[h] * (bc[h] * egc[h])], axis=1)
                                for h in heads])
        lhs1 = [jnp.concatenate([sol[h][:, HEAD_DIM:], q[h] * egc[h]], axis=0) for h in heads]
        kdt = [kt[h] * jnp.exp(g_last[h] - gr[h]) for h in heads]
        s = [s_ref[h] for h in heads]
        ws_qs = _mm16_each(lhs1, s)
        v_new = [sol[h][:, :HEAD_DIM] - ws_qs[h][:CHUNK] for h in heads]
        av_ds = _mm16_each([jnp.concatenate([a_qk[h], kdt[h]], axis=0) for h in heads], v_new)
        for h in heads:
            s_ref[h] = s[h] * jnp.exp(g_last[h]) + av_ds[h][CHUNK:]
            o = ws_qs[h][CHUNK:] + av_ds[h][:CHUNK]
            o = o * lax.rsqrt(jnp.mean(o * o, -1, keepdims=True) + 1e-6) * nw
            zz = z_ref[rs, cs[h]].astype(F32)
            o_ref[rs, cs[h]] = (o * (zz * jax.nn.sigmoid(zz))).astype(o_ref.dtype)


def delta_net(proj, conv_w, layer, beta, gam, gamt, norm_w, *, hb=8, tb=256):
    t = proj.shape[0]
    tb = _pick(t, tb)
    w = hb * HEAD_DIM
    nq, nk, nv, nz = Q_OFF // w, K_OFF // w, V_OFF // w, Z_OFF // w
    pspec = lambda off: pl.BlockSpec((tb, w), lambda h, i: (i, off + h))
    cspec = lambda off: pl.BlockSpec((None, CONV_K, w), lambda h, i: (layer, 0, off + h))
    return pl.pallas_call(
        functools.partial(_delta_body, hb=hb, tb=tb),
        grid=(DN_HEADS // hb, t // tb),
        in_specs=[pspec(nq), pspec(nk), pspec(nv), pspec(nz),
                  cspec(nq), cspec(nk), cspec(nv),
                  pl.BlockSpec((tb, LANES), lambda h, i: (i, 0)),
                  pl.BlockSpec((tb, LANES), lambda h, i: (i, 0)),
                  pl.BlockSpec((DN_HEADS, tb), lambda h, i: (0, i)),
                  pl.BlockSpec((1, HEAD_DIM), lambda h, i: (0, 0))],
        out_specs=pl.BlockSpec((tb, w), lambda h, i: (i, h)),
        out_shape=jax.ShapeDtypeStruct((t, DN_V_W), BF16),
        scratch_shapes=[pltpu.VMEM((hb, HEAD_DIM, HEAD_DIM), F32),
                        pltpu.VMEM((3, SUBLANES, w), F32),
                        pltpu.VMEM((SUBLANES + tb, w), F32),
                        pltpu.VMEM((3, tb, w), F32)],
        compiler_params=_cparams(("arbitrary", "arbitrary"), 48),
        name="delta_net",
    )(proj, proj, proj, proj, conv_w, conv_w, conv_w, beta, gam, gamt,
      norm_w.astype(F32).reshape(1, HEAD_DIM))


def _swa_body(sink_ref, q_ref, kc_ref, kp_ref, vc_ref, vp_ref, o_ref):
    i = pl.program_id(0)
    qi = lax.broadcasted_iota(I32, (WINDOW, 2 * WINDOW), 0)
    kj = lax.broadcasted_iota(I32, (WINDOW, 2 * WINDOW), 1)
    dist = qi - kj + WINDOW
    first_key = jnp.where(i > 0, 0, WINDOW)
    valid = (dist >= 0) & (dist < WINDOW) & (kj >= first_key)
    distf = dist.astype(F32)
    scale = HEAD_DIM ** -0.5
    for h in range(SWA_KV_HEADS):
        hs = slice(h * HEAD_DIM, (h + 1) * HEAD_DIM)
        kk = jnp.concatenate([kp_ref[:, hs], kc_ref[:, hs]], axis=0)
        vv = jnp.concatenate([vp_ref[:, hs], vc_ref[:, hs]], axis=0)
        for g in range(SWA_GROUP):
            hq = h * SWA_GROUP + g
            slope = 2.0 ** (-8.0 * (hq + 1) / SWA_Q_HEADS)
            qs = slice(hq * HEAD_DIM, (hq + 1) * HEAD_DIM)
            s = lax.dot_general(q_ref[:, qs], kk, (((1,), (1,)), ((), ())),
                                preferred_element_type=F32) * scale
            logits = jnp.where(valid, s - slope * distf, NEG_INF)
            sink = sink_ref[hq]
            m = jnp.maximum(jnp.max(logits, -1, keepdims=True), sink)
            e = jnp.exp(logits - m)
            denom = jnp.sum(e, -1, keepdims=True) + jnp.exp(sink - m)
            p = (e / denom).astype(BF16)
            o_ref[:, qs] = jnp.dot(p, vv, preferred_element_type=F32).astype(o_ref.dtype)


def swa_attention(proj_sw, sinks):
    t = proj_sw.shape[0]
    nb = t // WINDOW
    kb, vb = SWA_Q_W // SWA_KV_W, SWA_Q_W // SWA_KV_W + 1
    cur = lambda blk: pl.BlockSpec((WINDOW, SWA_KV_W), lambda i: (i, blk))
    prev = lambda blk: pl.BlockSpec((WINDOW, SWA_KV_W), lambda i: (jnp.maximum(i - 1, 0), blk))
    return pl.pallas_call(
        _swa_body,
        grid=(nb,),
        in_specs=[pl.BlockSpec(memory_space=pltpu.SMEM),
                  pl.BlockSpec((WINDOW, SWA_Q_W), lambda i: (i, 0)),
                  cur(kb), prev(kb), cur(vb), prev(vb)],
        out_specs=pl.BlockSpec((WINDOW, SWA_Q_W), lambda i: (i, 0)),
        out_shape=jax.ShapeDtypeStruct((t, SWA_Q_W), BF16),
        compiler_params=_cparams(("parallel",), 32),
        name="swa",
    )(sinks.astype(F32), proj_sw, proj_sw, proj_sw, proj_sw, proj_sw)


def _router_body(x_ref, w_ref, idx_ref, gate_ref):
    logits = jnp.dot(x_ref[...], w_ref[...], precision=lax.Precision.HIGHEST,
                     preferred_element_type=F32)
    lane = lax.broadcasted_iota(I32, logits.shape, 1)
    neg = jnp.float32(-jnp.inf)
    l1 = jnp.where(lane < N_EXPERTS, logits, neg)
    m1 = jnp.max(l1, -1, keepdims=True)
    i1 = jnp.min(jnp.where(l1 == m1, lane, LANES), -1, keepdims=True)
    l2 = jnp.where(lane == i1, neg, l1)
    m2 = jnp.max(l2, -1, keepdims=True)
    i2 = jnp.min(jnp.where(l2 == m2, lane, LANES), -1, keepdims=True)
    e = jnp.exp(m2 - m1)
    w1 = 1.0 / (1.0 + e)
    w2 = e * w1
    idx_ref[...] = jnp.where(lane == 0, i1, jnp.where(lane == 1, i2, 0))
    gate_ref[...] = jnp.where(lane == 0, w1, jnp.where(lane == 1, w2, 0.0))


def moe_router(x, router_w, *, bm=256):
    t, d = x.shape
    bm = _pick(t, bm)
    w = jnp.zeros((d, LANES), F32).at[:, :N_EXPERTS].set(router_w.astype(F32))
    return pl.pallas_call(
        _router_body,
        grid=(t // bm,),
        in_specs=[pl.BlockSpec((bm, d), lambda i: (i, 0)),
                  pl.BlockSpec((d, LANES), lambda i: (0, 0))],
        out_specs=[pl.BlockSpec((bm, LANES), lambda i: (i, 0)),
                   pl.BlockSpec((bm, LANES), lambda i: (i, 0))],
        out_shape=[jax.ShapeDtypeStruct((t, LANES), I32), jax.ShapeDtypeStruct((t, LANES), F32)],
        compiler_params=_cparams(("parallel",), 32),
        name="moe_router",
    )(x, w)


def _route_plan(top_i, tm):
    t = top_i.shape[0]
    n_assign = 2 * t
    e_flat = top_i.reshape(-1)
    onehot = (e_flat[:, None] == jnp.arange(N_EXPERTS, dtype=I32)[None, :]).astype(I32)
    csum = jnp.cumsum(onehot, axis=0)
    rank = jnp.sum(csum * onehot, axis=1) - 1
    counts = csum[-1]
    padded = ((counts + tm - 1) // tm) * tm
    ends = jnp.cumsum(padded)
    offs = ends - padded
    pos = offs[e_flat] + rank
    n_tiles = n_assign // tm + N_EXPERTS
    row_src = jnp.zeros((n_tiles * tm,), I32).at[pos].set(jnp.arange(n_assign, dtype=I32) // 2)
    tile_start = jnp.arange(n_tiles, dtype=I32) * tm
    tile_e = jnp.sum((tile_start[:, None] >= ends[None, :]).astype(I32), axis=1)
    tile_valid = (tile_start < ends[-1]).astype(I32)
    last_e = jnp.max(jnp.where(padded > 0, jnp.arange(N_EXPERTS, dtype=I32), 0))
    tile_e = jnp.minimum(tile_e, last_e)
    return row_src, pos[0::2], pos[1::2], tile_e, tile_valid, n_tiles


_ISSUE_UNROLL = 8


def _gather_rows_body(src_ref, x_hbm, o_ref, buf, sem, *, rows):
    i = pl.program_id(0)
    slot = lax.rem(i, 2)

    def row_copy(step_slot, r, src_row):
        return pltpu.make_async_copy(x_hbm.at[pl.ds(src_row, 1), :],
                                     buf.at[step_slot, pl.ds(r, 1), :], sem.at[step_slot])

    def issue(step, step_slot):
        def body(r, carry):
            row_copy(step_slot, r, src_ref[step * rows + r]).start()
            return carry
        lax.fori_loop(0, rows, body, 0, unroll=_ISSUE_UNROLL)

    @pl.when(i == 0)
    def _():
        issue(0, 0)

    @pl.when(i + 1 < pl.num_programs(0))
    def _():
        issue(i + 1, 1 - slot)

    def drain(r, carry):
        row_copy(slot, r, 0).wait()
        return carry

    lax.fori_loop(0, rows, drain, 0, unroll=_ISSUE_UNROLL)
    o_ref[...] = buf[slot].astype(o_ref.dtype)


def gather_rows(x, row_src, *, rows=512):
    d = x.shape[1]
    p = row_src.shape[0]
    rows = _pick(p, rows)
    return pl.pallas_call(
        functools.partial(_gather_rows_body, rows=rows),
        grid_spec=pltpu.PrefetchScalarGridSpec(
            num_scalar_prefetch=1,
            grid=(p // rows,),
            in_specs=[pl.BlockSpec(memory_space=pl.ANY)],
            out_specs=pl.BlockSpec((rows, d), lambda i, src: (i, 0)),
            scratch_shapes=[pltpu.VMEM((2, rows, d), F32), pltpu.SemaphoreType.DMA((2,))]),
        out_shape=jax.ShapeDtypeStruct((p, d), BF16),
        compiler_params=_cparams(("arbitrary",), (rows * d * (2 * 4 + 2 * 2 + 4)) // MIB + 6),
        name="moe_gather",
    )(row_src, x)


def _new_panel(te_ref, i):
    return (i == 0) | (te_ref[i] != te_ref[jnp.maximum(i - 1, 0)])


def _expert_up_body(te_ref, tv_ref, x_ref, w1_ref, w3_ref, h_ref, w1b_ref, w3b_ref):
    i = pl.program_id(1)

    @pl.when(_new_panel(te_ref, i))
    def _():
        w1b_ref[...] = w1_ref[...].astype(BF16)
        w3b_ref[...] = w3_ref[...].astype(BF16)

    @pl.when(tv_ref[i] == 1)
    def _():
        x = x_ref[...]
        a = jnp.dot(x, w1b_ref[...], preferred_element_type=F32)
        b = jnp.dot(x, w3b_ref[...], preferred_element_type=F32)
        h_ref[...] = (a * jax.nn.sigmoid(a) * b).astype(h_ref.dtype)

    @pl.when(tv_ref[i] == 0)
    def _():
        h_ref[...] = jnp.zeros_like(h_ref)


def expert_up(xs, w1, w3, layer, tile_e, tile_valid, tm, *, bn=512):
    p, d = xs.shape
    f = w1.shape[3]
    bn = _pick(f, bn)
    wspec = _resident((None, None, d, bn), lambda j, i, te, tv: (layer, te[i], 0, j))
    vmem = 2 * d * bn * (4 + 2) + 2 * (tm * d * 2 + tm * bn * 2) + 4 * tm * bn * 4
    return pl.pallas_call(
        _expert_up_body,
        grid_spec=pltpu.PrefetchScalarGridSpec(
            num_scalar_prefetch=2,
            grid=(f // bn, p // tm),
            in_specs=[pl.BlockSpec((tm, d), lambda j, i, te, tv: (i, 0)), wspec, wspec],
            out_specs=pl.BlockSpec((tm, bn), lambda j, i, te, tv: (i, j)),
            scratch_shapes=[pltpu.VMEM((d, bn), BF16), pltpu.VMEM((d, bn), BF16)]),
        out_shape=jax.ShapeDtypeStruct((p, f), BF16),
        compiler_params=_cparams(("arbitrary", "arbitrary"), vmem // MIB + 6),
        name="expert_up",
    )(tile_e, tile_valid, xs, w1, w3)


def _expert_down_body(te_ref, tv_ref, h_ref, w2_ref, y_ref, w2b_ref):
    i = pl.program_id(1)

    @pl.when(_new_panel(te_ref, i))
    def _():
        w2b_ref[...] = w2_ref[...].astype(BF16)

    @pl.when(tv_ref[i] == 1)
    def _():
        y_ref[...] = jnp.dot(h_ref[...], w2b_ref[...], preferred_element_type=F32).astype(y_ref.dtype)

    @pl.when(tv_ref[i] == 0)
    def _():
        y_ref[...] = jnp.zeros_like(y_ref)


def expert_down(h, w2, layer, tile_e, tile_valid, tm, *, bn=1024):
    p, f = h.shape
    d = w2.shape[3]
    bn = _pick(d, bn)
    vmem = f * bn * (4 + 2) + 2 * (tm * f * 2 + tm * bn * 4) + 2 * tm * bn * 4
    return pl.pallas_call(
        _expert_down_body,
        grid_spec=pltpu.PrefetchScalarGridSpec(
            num_scalar_prefetch=2,
            grid=(d // bn, p // tm),
            in_specs=[pl.BlockSpec((tm, f), lambda j, i, te, tv: (i, 0)),
                      _resident((None, None, f, bn), lambda j, i, te, tv: (layer, te[i], 0, j))],
            out_specs=pl.BlockSpec((tm, bn), lambda j, i, te, tv: (i, j)),
            scratch_shapes=[pltpu.VMEM((f, bn), BF16)]),
        out_shape=jax.ShapeDtypeStruct((p, d), F32),
        compiler_params=_cparams(("arbitrary", "arbitrary"), vmem // MIB + 6),
        name="expert_down",
    )(tile_e, tile_valid, h, w2)


def _combine_ln_body(p1_ref, p2_ref, y_hbm, x_ref, gate_ref, g_ref, b_ref, of_ref, ob_ref,
                     buf, sem, *, rows, alpha):
    i = pl.program_id(0)
    slot = lax.rem(i, 2)

    def row_copy(step_slot, which, r, src_row):
        return pltpu.make_async_copy(y_hbm.at[pl.ds(src_row, 1), :],
                                     buf.at[step_slot, which, pl.ds(r, 1), :], sem.at[step_slot])

    def issue(step, step_slot):
        def body(r, carry):
            row_copy(step_slot, 0, r, p1_ref[step * rows + r]).start()
            row_copy(step_slot, 1, r, p2_ref[step * rows + r]).start()
            return carry
        lax.fori_loop(0, rows, body, 0, unroll=_ISSUE_UNROLL)

    @pl.when(i == 0)
    def _():
        issue(0, 0)

    @pl.when(i + 1 < pl.num_programs(0))
    def _():
        issue(i + 1, 1 - slot)

    def drain(r, carry):
        row_copy(slot, 0, r, 0).wait()
        row_copy(slot, 1, r, 0).wait()
        return carry

    lax.fori_loop(0, rows, drain, 0, unroll=_ISSUE_UNROLL)
    gates = gate_ref[...]
    f = gates[:, 0:1] * buf[slot, 0] + gates[:, 1:2] * buf[slot, 1]
    o = _layer_norm_rows(alpha * x_ref[...] + f, g_ref[...], b_ref[...])
    of_ref[...] = o
    ob_ref[...] = o.astype(BF16)


def moe_combine_ln(y, pos1, pos2, gates, x, g, b, alpha, *, rows=128):
    t, d = x.shape
    rows = _pick(t, rows)
    row = lambda width: pl.BlockSpec((rows, width), lambda i, p1, p2: (i, 0))
    vec = pl.BlockSpec((1, d), lambda i, p1, p2: (0, 0))
    return pl.pallas_call(
        functools.partial(_combine_ln_body, rows=rows, alpha=alpha),
        grid_spec=pltpu.PrefetchScalarGridSpec(
            num_scalar_prefetch=2,
            grid=(t // rows,),
            in_specs=[pl.BlockSpec(memory_space=pl.ANY), row(d), row(LANES), vec, vec],
            out_specs=[row(d), row(d)],
            scratch_shapes=[pltpu.VMEM((2, 2, rows, d), F32), pltpu.SemaphoreType.DMA((2,))]),
        out_shape=[jax.ShapeDtypeStruct((t, d), F32), jax.ShapeDtypeStruct((t, d), BF16)],
        compiler_params=_cparams(("arbitrary",), (rows * d * (16 + 2 * (4 + 4 + 2) + 8)) // MIB + 6),
        name="moe_combine_ln",
    )(pos1, pos2, y, x, gates, g.reshape(1, d), b.reshape(1, d))


def moe_ffn_ln(x_f32, router_w, w1, w3, w2, layer, g, b, alpha, *, tm=512):
    idx, gates = moe_router(x_f32, router_w)
    row_src, pos1, pos2, tile_e, tile_valid, _ = _route_plan(idx[:, :2], tm)
    xs = gather_rows(x_f32, row_src, rows=tm)
    h = expert_up(xs, w1, w3, layer, tile_e, tile_valid, tm)
    y = expert_down(h, w2, layer, tile_e, tile_valid, tm)
    return moe_combine_ln(y, pos1, pos2, gates, x_f32, g, b, alpha)


def _small_in_weights(w_in_l):
    d = w_in_l.shape[0]
    w_ba = jnp.zeros((d, 2 * LANES), BF16)
    w_ba = w_ba.at[:, :DN_HEADS].set(w_in_l[:, _IN_Z:_IN_B].astype(BF16))
    w_ba = w_ba.at[:, LANES:LANES + DN_HEADS].set(w_in_l[:, _IN_B:_IN_A].astype(BF16))
    w_at = w_in_l[:, _IN_B:_IN_A].T.astype(BF16)
    w_sw = w_in_l[:, _IN_A:][None]
    return w_ba, w_at, w_sw


def kernel(x, w_in, conv_w, a_log, dt_bias, dn_norm_w, sinks, w_o, ln1_g, ln1_b, ffn_w1, ffn_w3,
           ffn_w2, router_w, exp_w1, exp_w3, exp_w2, ln2_g, ln2_b):
    bsz, seq, d = x.shape
    depth = w_in.shape[0]
    alpha = (2 * depth) ** 0.25
    w_in, w_o, conv_w = w_in.astype(F32), w_o.astype(F32), conv_w.astype(F32)
    outs = []
    for bi in range(bsz):
        xf = x[bi].astype(F32)
        xb = xf.astype(BF16)
        for i in range(depth):
            w_ba, w_at, w_sw = _small_in_weights(w_in[i])
            proj_dn = matmul_ws([xb], w_in, i, BF16, n_cols=_IN_Z, name="in_proj_dn")
            proj_sw = matmul_ws([xb], w_sw, 0, BF16, name="in_proj_sw")
            beta, gam, gamt = dn_gates(xb, w_ba, w_at, a_log[i], dt_bias[i])
            out_a = delta_net(proj_dn, conv_w, i, beta, gam, gamt, dn_norm_w[i])
            out_b = swa_attention(proj_sw, sinks[i])
            mix = matmul_ws([out_a, out_b], w_o, i, F32, name="out_proj")
            xf, xb = add_layer_norm(xf, mix, ln1_g[i], ln1_b[i], alpha)
            j = i // 2
            if i % 2 == 0:
                h = swiglu_up(xb, ffn_w1.astype(F32), ffn_w3.astype(F32), j)
                f = matmul_ws([h], ffn_w2.astype(F32), j, F32, bm=512, bn=512, name="ffn_down")
                xf, xb = add_layer_norm(xf, f, ln2_g[i], ln2_b[i], alpha)
            else:
                xf, xb = moe_ffn_ln(xf, router_w[j], exp_w1.astype(F32), exp_w3.astype(F32),
                                    exp_w2.astype(F32), j, ln2_g[i], ln2_b[i], alpha)
        outs.append(xf)
    return jnp.stack(outs, axis=0).astype(x.dtype)
```

```python
import functools

import jax
import jax.numpy as jnp
from jax import lax
from jax.experimental import pallas as pl
from jax.experimental.pallas import tpu as pltpu

F32 = jnp.float32
BF16 = jnp.bfloat16
I32 = jnp.int32

HEAD_DIM = 128
DN_HEADS = 16
DN_QK_W = DN_HEADS * HEAD_DIM
DN_V_W = DN_HEADS * HEAD_DIM
CONV_K = 4
SWA_Q_HEADS = 16
SWA_KV_HEADS = 4
SWA_GROUP = SWA_Q_HEADS // SWA_KV_HEADS
SWA_Q_W = SWA_Q_HEADS * HEAD_DIM
SWA_KV_W = SWA_KV_HEADS * HEAD_DIM
WINDOW = 128
N_EXPERTS = 8
LN_EPS = 1e-5
NEG_INF = -1e30

_IN_QKV = 2 * DN_QK_W + DN_V_W
_IN_Z = _IN_QKV + DN_V_W
_IN_B = _IN_Z + DN_HEADS
_IN_A = _IN_B + DN_HEADS
Q_OFF, K_OFF, V_OFF, Z_OFF = 0, DN_QK_W, 2 * DN_QK_W, _IN_QKV
SW_W = SWA_Q_W + 2 * SWA_KV_W

LANES = 128
SUBLANES = 8
VMEM_BYTES_V7X = 64 * 1024 * 1024
MIB = 1024 * 1024

CHUNK = 128


def _cparams(semantics, vmem_mib):
    assert vmem_mib * MIB < VMEM_BYTES_V7X, vmem_mib
    return pltpu.CompilerParams(dimension_semantics=semantics, vmem_limit_bytes=vmem_mib * MIB)


def _pick(n, pref):
    t = min(n, pref)
    while n % t:
        t //= 2
    return t


def _resident(block_shape, index_map):
    return pl.BlockSpec(block_shape, index_map, pipeline_mode=pl.Buffered(1))


def _mm_ws_body(*refs, n_parts, part_k):
    a_refs = refs[:n_parts]
    w_ref, o_ref, wb_ref = refs[n_parts:]

    @pl.when(pl.program_id(1) == 0)
    def _():
        wb_ref[...] = w_ref[...].astype(BF16)

    acc = jnp.dot(a_refs[0][...], wb_ref[0:part_k, :], preferred_element_type=F32)
    for p in range(1, n_parts):
        acc = acc + jnp.dot(a_refs[p][...], wb_ref[p * part_k:(p + 1) * part_k, :],
                            preferred_element_type=F32)
    o_ref[...] = acc.astype(o_ref.dtype)


def matmul_ws(a_parts, w, layer, out_dtype, *, n_cols=None, bm=512, bn=1024, name="mm"):
    m, part_k = a_parts[0].shape
    n_parts = len(a_parts)
    k = part_k * n_parts
    assert w.shape[1] == k
    n_cols = w.shape[2] if n_cols is None else n_cols
    bm, bn = _pick(m, bm), _pick(n_cols, bn)
    osz = jnp.dtype(out_dtype).itemsize
    vmem = k * bn * (4 + 2) + 2 * (bm * k * 2 + bm * bn * osz) + 2 * bm * bn * 4
    return pl.pallas_call(
        functools.partial(_mm_ws_body, n_parts=n_parts, part_k=part_k),
        grid=(n_cols // bn, m // bm),
        in_specs=[pl.BlockSpec((bm, part_k), lambda j, i: (i, 0)) for _ in range(n_parts)]
        + [_resident((None, k, bn), lambda j, i: (layer, 0, j))],
        out_specs=pl.BlockSpec((bm, bn), lambda j, i: (i, j)),
        out_shape=jax.ShapeDtypeStruct((m, n_cols), out_dtype),
        scratch_shapes=[pltpu.VMEM((k, bn), BF16)],
        compiler_params=_cparams(("arbitrary", "arbitrary"), vmem // MIB + 6),
        name=name,
    )(*a_parts, w)


def _in_proj_body(a_ref, wt_ref, o_ref, wb_ref):
    @pl.when(pl.program_id(1) == 0)
    def _():
        wb_ref[...] = wt_ref[...].astype(BF16)

    o_ref[...] = lax.dot_general(a_ref[...], wb_ref[...], (((1,), (1,)), ((), ())),
                                 preferred_element_type=F32).astype(o_ref.dtype)


def _in_proj_qkv_body(a_ref, wt_ref, cw_ref, o_ref, wb_ref, ext_ref, *, bm, bn, q_panels, qk_panels):
    j = pl.program_id(0)
    i = pl.program_id(1)
    halo = SUBLANES

    @pl.when(i == 0)
    def _():
        wb_ref[...] = wt_ref[...].astype(BF16)
        ext_ref[0:halo, :] = jnp.zeros((halo, bn), F32)

    acc = lax.dot_general(a_ref[...], wb_ref[...], (((1,), (1,)), ((), ())),
                          preferred_element_type=F32)
    ext_ref[halo:halo + bm, :] = acc
    w = cw_ref[...]
    y = acc * w[CONV_K - 1:CONV_K, :]
    for tap in range(CONV_K - 1):
        y = y + ext_ref[pl.ds(halo - (CONV_K - 1) + tap, bm), :] * w[tap:tap + 1, :]
    ext_ref[0:halo, :] = acc[bm - halo:, :]
    y = y * jax.nn.sigmoid(y)
    q_scale = jnp.where(j < q_panels, HEAD_DIM ** -0.5, 1.0)
    for h in range(bn // HEAD_DIM):
        hs = slice(h * HEAD_DIM, (h + 1) * HEAD_DIM)
        yh = y[:, hs]
        inv = lax.rsqrt(jnp.sum(yh * yh, -1, keepdims=True) + 1e-6) * q_scale
        o_ref[:, hs] = (yh * jnp.where(j < qk_panels, inv, 1.0)).astype(o_ref.dtype)


def in_proj(a, wt, layer, *, row0=0, n_cols=None, conv_w=None, bm=1024, bn=512, name="in_proj"):
    m, k = a.shape
    n_cols = wt.shape[1] if n_cols is None else n_cols
    bm, bn = _pick(m, bm), _pick(n_cols, bn)
    assert row0 % bn == 0
    vmem = 2 * bn * k * 4 + bn * k * 2 + 2 * (bm * k * 2 + bm * bn * 2) + bm * bn * 4
    a_spec = pl.BlockSpec((bm, k), lambda j, i: (i, 0))
    w_spec = pl.BlockSpec((None, bn, k), lambda j, i: (layer, row0 // bn + j, 0))
    common = dict(
        grid=(n_cols // bn, m // bm),
        out_specs=pl.BlockSpec((bm, bn), lambda j, i: (i, j)),
        out_shape=jax.ShapeDtypeStruct((m, n_cols), BF16),
        name=name)
    if conv_w is None:
        return pl.pallas_call(
            _in_proj_body, in_specs=[a_spec, w_spec],
            scratch_shapes=[pltpu.VMEM((bn, k), BF16)],
            compiler_params=_cparams(("arbitrary", "arbitrary"), vmem // MIB + 6), **common)(a, wt)
    assert n_cols == _IN_QKV and DN_QK_W % bn == 0 and bn % HEAD_DIM == 0
    vmem += (SUBLANES + bm) * bn * 4 + 2 * bm * bn * 4
    return pl.pallas_call(
        functools.partial(_in_proj_qkv_body, bm=bm, bn=bn, q_panels=DN_QK_W // bn,
                          qk_panels=2 * DN_QK_W // bn),
        in_specs=[a_spec, w_spec, pl.BlockSpec((None, CONV_K, bn), lambda j, i: (layer, 0, j))],
        scratch_shapes=[pltpu.VMEM((bn, k), BF16), pltpu.VMEM((SUBLANES + bm, bn), F32)],
        compiler_params=_cparams(("arbitrary", "arbitrary"), vmem // MIB + 6), **common)(a, wt, conv_w)


def _layer_norm_rows(r, g, b):
    mu = jnp.mean(r, -1, keepdims=True)
    c = r - mu
    var = jnp.mean(c * c, -1, keepdims=True)
    return c * lax.rsqrt(var + LN_EPS) * g + b


def _add_ln_body(x_ref, y_ref, g_ref, b_ref, of_ref, ob_ref, *, alpha):
    r = alpha * x_ref[...] + y_ref[...].astype(F32)
    o = _layer_norm_rows(r, g_ref[...], b_ref[...])
    of_ref[...] = o
    ob_ref[...] = o.astype(BF16)


def add_layer_norm(x, y, g, b, alpha, *, bm=256):
    t, d = x.shape
    bm = _pick(t, bm)
    row = pl.BlockSpec((bm, d), lambda i: (i, 0))
    vec = pl.BlockSpec((1, d), lambda i: (0, 0))
    vmem_mib = (2 * bm * d * (4 + y.dtype.itemsize + 4 + 2)) // MIB + 8
    return pl.pallas_call(
        functools.partial(_add_ln_body, alpha=alpha),
        grid=(t // bm,),
        in_specs=[row, row, vec, vec],
        out_specs=[row, row],
        out_shape=[jax.ShapeDtypeStruct((t, d), F32), jax.ShapeDtypeStruct((t, d), BF16)],
        compiler_params=_cparams(("parallel",), vmem_mib),
        name="add_ln",
    )(x, y, g.reshape(1, d), b.reshape(1, d))


def _swiglu_up_body(x_ref, w1_ref, w3_ref, h_ref, w1b_ref, w3b_ref):
    @pl.when(pl.program_id(1) == 0)
    def _():
        w1b_ref[...] = w1_ref[...].astype(BF16)
        w3b_ref[...] = w3_ref[...].astype(BF16)

    x = x_ref[...]
    a = jnp.dot(x, w1b_ref[...], preferred_element_type=F32)
    b = jnp.dot(x, w3b_ref[...], preferred_element_type=F32)
    h_ref[...] = (a * jax.nn.sigmoid(a) * b).astype(h_ref.dtype)


def swiglu_up(x, w1, w3, layer, *, bm=1024, bn=512):
    t, d = x.shape
    f = w1.shape[2]
    bm, bn = _pick(t, bm), _pick(f, bn)
    wspec = _resident((None, d, bn), lambda j, i: (layer, 0, j))
    vmem = 2 * d * bn * (4 + 2) + 2 * (bm * d * 2 + bm * bn * 2) + 4 * bm * bn * 4
    return pl.pallas_call(
        _swiglu_up_body,
        grid=(f // bn, t // bm),
        in_specs=[pl.BlockSpec((bm, d), lambda j, i: (i, 0)), wspec, wspec],
        out_specs=pl.BlockSpec((bm, bn), lambda j, i: (i, j)),
        out_shape=jax.ShapeDtypeStruct((t, f), BF16),
        scratch_shapes=[pltpu.VMEM((d, bn), BF16), pltpu.VMEM((d, bn), BF16)],
        compiler_params=_cparams(("arbitrary", "arbitrary"), vmem // MIB + 6),
        name="swiglu_up",
    )(x, w1, w3)


def _softplus(x):
    return jnp.maximum(x, 0.0) + jnp.log(1.0 + jnp.exp(-jnp.abs(x)))


def _dn_gate_body(x_ref, wba_ref, wat_ref, alog_r_ref, dtb_r_ref, alog_c_ref, dtb_c_ref,
                  beta_ref, gam_ref, gamt_ref, *, tb):
    x = x_ref[...]
    p = jnp.dot(x, wba_ref[...], preferred_element_type=F32)
    beta_ref[...] = jax.nn.sigmoid(p[:, :LANES])
    g = -jnp.exp(alog_r_ref[...]) * _softplus(p[:, LANES:] + dtb_r_ref[...])
    pt = lax.dot_general(wat_ref[...], x, (((1,), (1,)), ((), ())),
                         preferred_element_type=F32)
    gt = -jnp.exp(alog_c_ref[...]) * _softplus(pt + dtb_c_ref[...])
    r = lax.broadcasted_iota(I32, (tb, tb), 0)
    c = lax.broadcasted_iota(I32, (tb, tb), 1)
    same = (r // CHUNK) == (c // CHUNK)
    lower = jnp.where(same & (r >= c), 1.0, 0.0).astype(F32)
    upper = jnp.where(same & (r <= c), 1.0, 0.0).astype(F32)
    gam_ref[...] = jnp.dot(lower, g, precision=lax.Precision.HIGHEST, preferred_element_type=F32)
    gamt_ref[...] = jnp.dot(gt, upper, precision=lax.Precision.HIGHEST, preferred_element_type=F32)


def dn_gates(x, wba, wat, a_log, dt_bias, *, tb=512):
    t, d = x.shape
    tb = _pick(t, tb)
    pad = lambda v: jnp.zeros((1, LANES), F32).at[0, :DN_HEADS].set(v.astype(F32))
    col = lambda v: v.astype(F32).reshape(DN_HEADS, 1)
    full = lambda shape: pl.BlockSpec(shape, lambda i: (0, 0))
    return pl.pallas_call(
        functools.partial(_dn_gate_body, tb=tb),
        grid=(t // tb,),
        in_specs=[pl.BlockSpec((tb, d), lambda i: (i, 0)),
                  full((d, 2 * LANES)), full((DN_HEADS, d)),
                  full((1, LANES)), full((1, LANES)), full((DN_HEADS, 1)), full((DN_HEADS, 1))],
        out_specs=[pl.BlockSpec((tb, LANES), lambda i: (i, 0)),
                   pl.BlockSpec((tb, LANES), lambda i: (i, 0)),
                   pl.BlockSpec((DN_HEADS, tb), lambda i: (0, i))],
        out_shape=[jax.ShapeDtypeStruct((t, LANES), F32),
                   jax.ShapeDtypeStruct((t, LANES), F32),
                   jax.ShapeDtypeStruct((DN_HEADS, t), F32)],
        compiler_params=_cparams(("parallel",), 32),
        name="dn_gates",
    )(x, wba, wat, pad(a_log), pad(dt_bias), col(a_log), col(dt_bias))


def _mm16(a, b):
    return jnp.dot(a.astype(BF16), b.astype(BF16), preferred_element_type=F32)


def _mm16_each(xs, ys):
    return [_mm16(x, y) for x, y in zip(xs, ys)]


def _unit_lower_inverse_each(a_list, eye, diag_blocks):
    d = [jnp.where(diag_blocks, a, 0.0) for a in a_list]
    n = [a - di for a, di in zip(a_list, d)]
    d2 = _mm16_each(d, d)
    x = _mm16_each([eye - t for t in d], [eye + t for t in d2])
    d4 = _mm16_each(d2, d2)
    x = _mm16_each(x, [eye + t for t in d4])
    d8 = _mm16_each(d4, d4)
    x = _mm16_each(x, [eye + t for t in d8])
    m = _mm16_each(x, n)
    m2 = _mm16_each(m, m)
    y = _mm16_each([eye - t for t in m], [eye + t for t in m2])
    m4 = _mm16_each(m2, m2)
    y = _mm16_each(y, [eye + t for t in m4])
    return _mm16_each(y, x)


def _delta_body(q_ref, k_ref, v_ref, z_ref, beta_ref, gam_ref, gamt_ref, nw_ref, o_ref, s_ref, *,
                hb, tb):
    hg = pl.program_id(0)
    t = pl.program_id(1)

    @pl.when(t == 0)
    def _():
        s_ref[...] = jnp.zeros_like(s_ref)

    row = lax.broadcasted_iota(I32, (CHUNK, CHUNK), 0)
    col = lax.broadcasted_iota(I32, (CHUNK, CHUNK), 1)
    causal = row >= col
    strict = row > col
    diag_blocks = (row // 16) == (col // 16)
    eye = jnp.where(row == col, 1.0, 0.0).astype(F32)
    lane = lax.broadcasted_iota(I32, (tb, LANES), 1)
    nw = nw_ref[...]
    heads = range(hb)
    sq = (CHUNK, CHUNK)

    bcol, gcol, grow = [], [], []
    for hl in heads:
        head = hg * hb + hl
        sel = lane == head
        bcol.append(jnp.sum(jnp.where(sel, beta_ref[...], 0.0), axis=-1, keepdims=True))
        gcol.append(jnp.sum(jnp.where(sel, gam_ref[...], 0.0), axis=-1, keepdims=True))
        grow.append(gamt_ref[pl.ds(head, 1), :])

    for c in range(tb // CHUNK):
        rs = slice(c * CHUNK, (c + 1) * CHUNK)
        cs = [slice(hl * HEAD_DIM, (hl + 1) * HEAD_DIM) for hl in heads]
        q = [q_ref[rs, cs[h]].astype(F32) for h in heads]
        k = [k_ref[rs, cs[h]].astype(F32) for h in heads]
        v = [v_ref[rs, cs[h]].astype(F32) for h in heads]
        bc = [jnp.broadcast_to(bcol[h][rs, :], sq) for h in heads]
        gc = [jnp.broadcast_to(gcol[h][rs, :], sq) for h in heads]
        gr = [jnp.broadcast_to(grow[h][:, rs], sq) for h in heads]
        g_last = [jnp.broadcast_to(grow[h][:, (c + 1) * CHUNK - 1:(c + 1) * CHUNK], sq) for h in heads]
        decay = [jnp.exp(jnp.where(causal, gc[h] - gr[h], NEG_INF)) for h in heads]
        egc = [jnp.exp(gc[h]) for h in heads]
        kt = [k[h].T for h in heads]
        gram = _mm16_each([jnp.concatenate([q[h], k[h]], axis=0) for h in heads], kt)
        a_qk = [gram[h][:CHUNK] * decay[h] for h in heads]
        a_kk = [jnp.where(strict, bc[h] * gram[h][CHUNK:] * decay[h], 0.0) for h in heads]
        tinv = _unit_lower_inverse_each(a_kk, eye, diag_blocks)
        sol = _mm16_each(tinv, [jnp.concatenate([v[h] * bc[h], k[h] * (bc[h] * egc[h])], axis=1)
                                for h in heads])
        lhs1 = [jnp.concatenate([sol[h][:, HEAD_DIM:], q[h] * egc[h]], axis=0) for h in heads]
        kdt = [kt[h] * jnp.exp(g_last[h] - gr[h]) for h in heads]
        s = [s_ref[h] for h in heads]
        ws_qs = _mm16_each(lhs1, s)
        v_new = [sol[h][:, :HEAD_DIM] - ws_qs[h][:CHUNK] for h in heads]
        av_ds = _mm16_each([jnp.concatenate([a_qk[h], kdt[h]], axis=0) for h in heads], v_new)
        for h in heads:
            s_ref[h] = s[h] * jnp.exp(g_last[h]) + av_ds[h][CHUNK:]
            o = ws_qs[h][CHUNK:] + av_ds[h][:CHUNK]
            o = o * lax.rsqrt(jnp.mean(o * o, -1, keepdims=True) + 1e-6) * nw
            zz = z_ref[rs, cs[h]].astype(F32)
            o_ref[rs, cs[h]] = (o * (zz * jax.nn.sigmoid(zz))).astype(o_ref.dtype)


def delta_net(qkv, z, beta, gam, gamt, norm_w, *, hb=8, tb=256):
    t = qkv.shape[0]
    tb = _pick(t, tb)
    w = hb * HEAD_DIM
    pspec = lambda off: pl.BlockSpec((tb, w), lambda h, i: (i, off // w + h))
    return pl.pallas_call(
        functools.partial(_delta_body, hb=hb, tb=tb),
        grid=(DN_HEADS // hb, t // tb),
        in_specs=[pspec(Q_OFF), pspec(K_OFF), pspec(V_OFF), pspec(0),
                  pl.BlockSpec((tb, LANES), lambda h, i: (i, 0)),
                  pl.BlockSpec((tb, LANES), lambda h, i: (i, 0)),
                  pl.BlockSpec((DN_HEADS, tb), lambda h, i: (0, i)),
                  pl.BlockSpec((1, HEAD_DIM), lambda h, i: (0, 0))],
        out_specs=pl.BlockSpec((tb, w), lambda h, i: (i, h)),
        out_shape=jax.ShapeDtypeStruct((t, DN_V_W), BF16),
        scratch_shapes=[pltpu.VMEM((hb, HEAD_DIM, HEAD_DIM), F32)],
        compiler_params=_cparams(("arbitrary", "arbitrary"), 48),
        name="delta_net",
    )(qkv, qkv, qkv, z, beta, gam, gamt, norm_w.astype(F32).reshape(1, HEAD_DIM))


def _swa_body(sink_ref, q_ref, kc_ref, kp_ref, vc_ref, vp_ref, o_ref):
    i = pl.program_id(0)
    qi = lax.broadcasted_iota(I32, (WINDOW, 2 * WINDOW), 0)
    kj = lax.broadcasted_iota(I32, (WINDOW, 2 * WINDOW), 1)
    dist = qi - kj + WINDOW
    first_key = jnp.where(i > 0, 0, WINDOW)
    valid = (dist >= 0) & (dist < WINDOW) & (kj >= first_key)
    distf = dist.astype(F32)
    scale = HEAD_DIM ** -0.5
    for h in range(SWA_KV_HEADS):
        hs = slice(h * HEAD_DIM, (h + 1) * HEAD_DIM)
        kk = jnp.concatenate([kp_ref[:, hs], kc_ref[:, hs]], axis=0)
        vv = jnp.concatenate([vp_ref[:, hs], vc_ref[:, hs]], axis=0)
        vv1 = jnp.concatenate([vv, jnp.ones_like(vv)], axis=1)
        for g in range(SWA_GROUP):
            hq = h * SWA_GROUP + g
            slope = 2.0 ** (-8.0 * (hq + 1) / SWA_Q_HEADS)
            qs = slice(hq * HEAD_DIM, (hq + 1) * HEAD_DIM)
            s = lax.dot_general(q_ref[:, qs], kk, (((1,), (1,)), ((), ())),
                                preferred_element_type=F32) * scale
            logits = jnp.where(valid, s - slope * distf, NEG_INF)
            sink = sink_ref[hq]
            m = jnp.maximum(jnp.max(logits, -1, keepdims=True), sink)
            e = jnp.exp(logits - m).astype(BF16)
            ov = jnp.dot(e, vv1, preferred_element_type=F32)
            denom = ov[:, HEAD_DIM:] + jnp.exp(sink - m)
            o_ref[:, qs] = (ov[:, :HEAD_DIM] / denom).astype(o_ref.dtype)


def swa_attention(proj_sw, sinks):
    t = proj_sw.shape[0]
    nb = t // WINDOW
    kb, vb = SWA_Q_W // SWA_KV_W, SWA_Q_W // SWA_KV_W + 1
    cur = lambda blk: pl.BlockSpec((WINDOW, SWA_KV_W), lambda i: (i, blk))
    prev = lambda blk: pl.BlockSpec((WINDOW, SWA_KV_W), lambda i: (jnp.maximum(i - 1, 0), blk))
    return pl.pallas_call(
        _swa_body,
        grid=(nb,),
        in_specs=[pl.BlockSpec(memory_space=pltpu.SMEM),
                  pl.BlockSpec((WINDOW, SWA_Q_W), lambda i: (i, 0)),
                  cur(kb), prev(kb), cur(vb), prev(vb)],
        out_specs=pl.BlockSpec((WINDOW, SWA_Q_W), lambda i: (i, 0)),
        out_shape=jax.ShapeDtypeStruct((t, SWA_Q_W), BF16),
        compiler_params=_cparams(("parallel",), 32),
        name="swa",
    )(sinks.astype(F32), proj_sw, proj_sw, proj_sw, proj_sw, proj_sw)


def _router_body(x_ref, w_ref, idx_ref, gate_ref):
    logits = jnp.dot(x_ref[...], w_ref[...], precision=lax.Precision.HIGHEST,
                     preferred_element_type=F32)
    lane = lax.broadcasted_iota(I32, logits.shape, 1)
    neg = jnp.float32(-jnp.inf)
    l1 = jnp.where(lane < N_EXPERTS, logits, neg)
    m1 = jnp.max(l1, -1, keepdims=True)
    i1 = jnp.min(jnp.where(l1 == m1, lane, LANES), -1, keepdims=True)
    l2 = jnp.where(lane == i1, neg, l1)
    m2 = jnp.max(l2, -1, keepdims=True)
    i2 = jnp.min(jnp.where(l2 == m2, lane, LANES), -1, keepdims=True)
    e = jnp.exp(m2 - m1)
    w1 = 1.0 / (1.0 + e)
    w2 = e * w1
    idx_ref[...] = jnp.where(lane == 0, i1, jnp.where(lane == 1, i2, 0))
    gate_ref[...] = jnp.where(lane == 0, w1, jnp.where(lane == 1, w2, 0.0))


def moe_router(x, router_w, *, bm=256):
    t, d = x.shape
    bm = _pick(t, bm)
    w = jnp.zeros((d, LANES), F32).at[:, :N_EXPERTS].set(router_w.astype(F32))
    return pl.pallas_call(
        _router_body,
        grid=(t // bm,),
        in_specs=[pl.BlockSpec((bm, d), lambda i: (i, 0)),
                  pl.BlockSpec((d, LANES), lambda i: (0, 0))],
        out_specs=[pl.BlockSpec((bm, LANES), lambda i: (i, 0)),
                   pl.BlockSpec((bm, LANES), lambda i: (i, 0))],
        out_shape=[jax.ShapeDtypeStruct((t, LANES), I32), jax.ShapeDtypeStruct((t, LANES), F32)],
        compiler_params=_cparams(("parallel",), 32),
        name="moe_router",
    )(x, w)


def _route_plan(top_i, tm):
    t = top_i.shape[0]
    n_assign = 2 * t
    e_flat = top_i.reshape(-1)
    onehot = (e_flat[:, None] == jnp.arange(N_EXPERTS, dtype=I32)[None, :]).astype(I32)
    csum = jnp.cumsum(onehot, axis=0)
    rank = jnp.sum(csum * onehot, axis=1) - 1
    counts = csum[-1]
    padded = ((counts + tm - 1) // tm) * tm
    ends = jnp.cumsum(padded)
    offs = ends - padded
    pos = offs[e_flat] + rank
    n_tiles = n_assign // tm + N_EXPERTS
    row_src = jnp.zeros((n_tiles * tm,), I32).at[pos].set(jnp.arange(n_assign, dtype=I32) // 2)
    tile_start = jnp.arange(n_tiles, dtype=I32) * tm
    tile_e = jnp.sum((tile_start[:, None] >= ends[None, :]).astype(I32), axis=1)
    tile_valid = (tile_start < ends[-1]).astype(I32)
    last_e = jnp.max(jnp.where(padded > 0, jnp.arange(N_EXPERTS, dtype=I32), 0))
    tile_e = jnp.minimum(tile_e, last_e)
    return row_src, pos[0::2], pos[1::2], tile_e, tile_valid, n_tiles


_ISSUE_UNROLL = 8


def _gather_rows_body(src_ref, x_hbm, o_ref, buf, sem, *, rows):
    i = pl.program_id(0)
    slot = lax.rem(i, 2)

    def row_copy(step_slot, r, src_row):
        return pltpu.make_async_copy(x_hbm.at[pl.ds(src_row, 1), :],
                                     buf.at[step_slot, pl.ds(r, 1), :], sem.at[step_slot])

    def issue(step, step_slot):
        def body(r, carry):
            row_copy(step_slot, r, src_ref[step * rows + r]).start()
            return carry
        lax.fori_loop(0, rows, body, 0, unroll=_ISSUE_UNROLL)

    @pl.when(i == 0)
    def _():
        issue(0, 0)

    @pl.when(i + 1 < pl.num_programs(0))
    def _():
        issue(i + 1, 1 - slot)

    def drain(r, carry):
        row_copy(slot, r, 0).wait()
        return carry

    lax.fori_loop(0, rows, drain, 0, unroll=_ISSUE_UNROLL)
    o_ref[...] = buf[slot].astype(o_ref.dtype)


def gather_rows(x, row_src, *, rows=512):
    d = x.shape[1]
    p = row_src.shape[0]
    rows = _pick(p, rows)
    return pl.pallas_call(
        functools.partial(_gather_rows_body, rows=rows),
        grid_spec=pltpu.PrefetchScalarGridSpec(
            num_scalar_prefetch=1,
            grid=(p // rows,),
            in_specs=[pl.BlockSpec(memory_space=pl.ANY)],
            out_specs=pl.BlockSpec((rows, d), lambda i, src: (i, 0)),
            scratch_shapes=[pltpu.VMEM((2, rows, d), F32), pltpu.SemaphoreType.DMA((2,))]),
        out_shape=jax.ShapeDtypeStruct((p, d), BF16),
        compiler_params=_cparams(("arbitrary",), (rows * d * (2 * 4 + 2 * 2 + 4)) // MIB + 6),
        name="moe_gather",
    )(row_src, x)


def _new_panel(te_ref, i):
    return (i == 0) | (te_ref[i] != te_ref[jnp.maximum(i - 1, 0)])


def _expert_up_body(te_ref, tv_ref, x_ref, w1_ref, w3_ref, h_ref, w1b_ref, w3b_ref):
    i = pl.program_id(1)

    @pl.when(_new_panel(te_ref, i))
    def _():
        w1b_ref[...] = w1_ref[...].astype(BF16)
        w3b_ref[...] = w3_ref[...].astype(BF16)

    @pl.when(tv_ref[i] == 1)
    def _():
        x = x_ref[...]
        a = jnp.dot(x, w1b_ref[...], preferred_element_type=F32)
        b = jnp.dot(x, w3b_ref[...], preferred_element_type=F32)
        h_ref[...] = (a * jax.nn.sigmoid(a) * b).astype(h_ref.dtype)

    @pl.when(tv_ref[i] == 0)
    def _():
        h_ref[...] = jnp.zeros_like(h_ref)


def expert_up(xs, w1, w3, layer, tile_e, tile_valid, tm, *, bn=512):
    p, d = xs.shape
    f = w1.shape[3]
    bn = _pick(f, bn)
    wspec = _resident((None, None, d, bn), lambda j, i, te, tv: (layer, te[i], 0, j))
    vmem = 2 * d * bn * (4 + 2) + 2 * (tm * d * 2 + tm * bn * 2) + 4 * tm * bn * 4
    return pl.pallas_call(
        _expert_up_body,
        grid_spec=pltpu.PrefetchScalarGridSpec(
            num_scalar_prefetch=2,
            grid=(f // bn, p // tm),
            in_specs=[pl.BlockSpec((tm, d), lambda j, i, te, tv: (i, 0)), wspec, wspec],
            out_specs=pl.BlockSpec((tm, bn), lambda j, i, te, tv: (i, j)),
            scratch_shapes=[pltpu.VMEM((d, bn), BF16), pltpu.VMEM((d, bn), BF16)]),
        out_shape=jax.ShapeDtypeStruct((p, f), BF16),
        compiler_params=_cparams(("arbitrary", "arbitrary"), vmem // MIB + 6),
        name="expert_up",
    )(tile_e, tile_valid, xs, w1, w3)


def _expert_down_body(te_ref, tv_ref, h_ref, w2_ref, y_ref, w2b_ref):
    i = pl.program_id(1)

    @pl.when(_new_panel(te_ref, i))
    def _():
        w2b_ref[...] = w2_ref[...].astype(BF16)

    @pl.when(tv_ref[i] == 1)
    def _():
        y_ref[...] = jnp.dot(h_ref[...], w2b_ref[...], preferred_element_type=F32).astype(y_ref.dtype)

    @pl.when(tv_ref[i] == 0)
    def _():
        y_ref[...] = jnp.zeros_like(y_ref)


def expert_down(h, w2, layer, tile_e, tile_valid, tm, *, bn=1024):
    p, f = h.shape
    d = w2.shape[3]
    bn = _pick(d, bn)
    vmem = f * bn * (2 * 4 + 2) + 2 * (tm * f * 2 + tm * bn * 4) + 2 * tm * bn * 4
    return pl.pallas_call(
        _expert_down_body,
        grid_spec=pltpu.PrefetchScalarGridSpec(
            num_scalar_prefetch=2,
            grid=(d // bn, p // tm),
            in_specs=[pl.BlockSpec((tm, f), lambda j, i, te, tv: (i, 0)),
                      pl.BlockSpec((None, None, f, bn), lambda j, i, te, tv: (layer, te[i], 0, j))],
            out_specs=pl.BlockSpec((tm, bn), lambda j, i, te, tv: (i, j)),
            scratch_shapes=[pltpu.VMEM((f, bn), BF16)]),
        out_shape=jax.ShapeDtypeStruct((p, d), F32),
        compiler_params=_cparams(("arbitrary", "arbitrary"), vmem // MIB + 6),
        name="expert_down",
    )(tile_e, tile_valid, h, w2)


def _combine_ln_body(p1_ref, p2_ref, y_hbm, x_ref, gate_ref, g_ref, b_ref, of_ref, ob_ref,
                     buf, sem, *, rows, alpha):
    i = pl.program_id(0)
    slot = lax.rem(i, 2)

    def row_copy(step_slot, which, r, src_row):
        return pltpu.make_async_copy(y_hbm.at[pl.ds(src_row, 1), :],
                                     buf.at[step_slot, which, pl.ds(r, 1), :], sem.at[step_slot])

    def issue(step, step_slot):
        def body(r, carry):
            row_copy(step_slot, 0, r, p1_ref[step * rows + r]).start()
            row_copy(step_slot, 1, r, p2_ref[step * rows + r]).start()
            return carry
        lax.fori_loop(0, rows, body, 0, unroll=_ISSUE_UNROLL)

    @pl.when(i == 0)
    def _():
        issue(0, 0)

    @pl.when(i + 1 < pl.num_programs(0))
    def _():
        issue(i + 1, 1 - slot)

    def drain(r, carry):
        row_copy(slot, 0, r, 0).wait()
        row_copy(slot, 1, r, 0).wait()
        return carry

    lax.fori_loop(0, rows, drain, 0, unroll=_ISSUE_UNROLL)
    gates = gate_ref[...]
    f = gates[:, 0:1] * buf[slot, 0] + gates[:, 1:2] * buf[slot, 1]
    o = _layer_norm_rows(alpha * x_ref[...] + f, g_ref[...], b_ref[...])
    of_ref[...] = o
    ob_ref[...] = o.astype(BF16)


def moe_combine_ln(y, pos1, pos2, gates, x, g, b, alpha, *, rows=128):
    t, d = x.shape
    rows = _pick(t, rows)
    row = lambda width: pl.BlockSpec((rows, width), lambda i, p1, p2: (i, 0))
    vec = pl.BlockSpec((1, d), lambda i, p1, p2: (0, 0))
    return pl.pallas_call(
        functools.partial(_combine_ln_body, rows=rows, alpha=alpha),
        grid_spec=pltpu.PrefetchScalarGridSpec(
            num_scalar_prefetch=2,
            grid=(t // rows,),
            in_specs=[pl.BlockSpec(memory_space=pl.ANY), row(d), row(LANES), vec, vec],
            out_specs=[row(d), row(d)],
            scratch_shapes=[pltpu.VMEM((2, 2, rows, d), F32), pltpu.SemaphoreType.DMA((2,))]),
        out_shape=[jax.ShapeDtypeStruct((t, d), F32), jax.ShapeDtypeStruct((t, d), BF16)],
        compiler_params=_cparams(("arbitrary",), (rows * d * (16 + 2 * (4 + 4 + 2) + 8)) // MIB + 6),
        name="moe_combine_ln",
    )(pos1, pos2, y, x, gates, g.reshape(1, d), b.reshape(1, d))


def moe_ffn_ln(x_f32, router_w, w1, w3, w2, layer, g, b, alpha, *, tm=512):
    idx, gates = moe_router(x_f32, router_w)
    row_src, pos1, pos2, tile_e, tile_valid, _ = _route_plan(idx[:, :2], tm)
    xs = gather_rows(x_f32, row_src, rows=tm)
    h = expert_up(xs, w1, w3, layer, tile_e, tile_valid, tm)
    y = expert_down(h, w2, layer, tile_e, tile_valid, tm)
    return moe_combine_ln(y, pos1, pos2, gates, x_f32, g, b, alpha)


def _gate_weights(w_in, layer):
    d = w_in.shape[1]
    w_b = w_in[layer, :, _IN_Z:_IN_B].astype(BF16)
    w_a = w_in[layer, :, _IN_B:_IN_A].astype(BF16)
    w_ba = jnp.zeros((d, 2 * LANES), BF16).at[:, :DN_HEADS].set(w_b).at[:, LANES:LANES + DN_HEADS].set(w_a)
    return w_ba, w_a.T


def kernel(x, w_in, conv_w, a_log, dt_bias, dn_norm_w, sinks, w_o, ln1_g, ln1_b, ffn_w1, ffn_w3,
           ffn_w2, router_w, exp_w1, exp_w3, exp_w2, ln2_g, ln2_b):
    bsz, seq, d = x.shape
    depth = w_in.shape[0]
    alpha = (2 * depth) ** 0.25
    w_in, w_o, conv_w = w_in.astype(F32), w_o.astype(F32), conv_w.astype(F32)
    wt_in = jnp.swapaxes(w_in, 1, 2)
    wt_sw = wt_in[:, _IN_A:, :]
    outs = []
    for bi in range(bsz):
        xf = x[bi].astype(F32)
        xb = xf.astype(BF16)
        for i in range(depth):
            w_ba, w_at = _gate_weights(w_in, i)
            qkv = in_proj(xb, wt_in, i, n_cols=_IN_QKV, conv_w=conv_w, name="in_proj_qkv")
            z = in_proj(xb, wt_in, i, row0=_IN_QKV, n_cols=DN_V_W, name="in_proj_z")
            proj_sw = in_proj(xb, wt_sw, i, name="in_proj_sw")
            beta, gam, gamt = dn_gates(xb, w_ba, w_at, a_log[i], dt_bias[i])
            out_a = delta_net(qkv, z, beta, gam, gamt, dn_norm_w[i])
            out_b = swa_attention(proj_sw, sinks[i])
            mix = matmul_ws([out_a, out_b], w_o, i, F32, name="out_proj")
            xf, xb = add_layer_norm(xf, mix, ln1_g[i], ln1_b[i], alpha)
            j = i // 2
            if i % 2 == 0:
                h = swiglu_up(xb, ffn_w1.astype(F32), ffn_w3.astype(F32), j)
                f = matmul_ws([h], ffn_w2.astype(F32), j, F32, bm=512, bn=512, name="ffn_down")
                xf, xb = add_layer_norm(xf, f, ln2_g[i], ln2_b[i], alpha)
            else:
                xf, xb = moe_ffn_ln(xf, router_w[j], exp_w1.astype(F32), exp_w3.astype(F32),
                                    exp_w2.astype(F32), j, ln2_g[i], ln2_b[i], alpha)
        outs.append(xf)
    return jnp.stack(outs, axis=0).astype(x.dtype)
```

```python
import functools

import jax
import jax.numpy as jnp
from jax import lax
from jax.experimental import pallas as pl
from jax.experimental.pallas import tpu as pltpu

F32 = jnp.float32
BF16 = jnp.bfloat16
I32 = jnp.int32

HEAD_DIM = 128
DN_HEADS = 16
DN_QK_W = DN_HEADS * HEAD_DIM
DN_V_W = DN_HEADS * HEAD_DIM
CONV_K = 4
SWA_Q_HEADS = 16
SWA_KV_HEADS = 4
SWA_GROUP = SWA_Q_HEADS // SWA_KV_HEADS
SWA_Q_W = SWA_Q_HEADS * HEAD_DIM
SWA_KV_W = SWA_KV_HEADS * HEAD_DIM
WINDOW = 128
N_EXPERTS = 8
LN_EPS = 1e-5
NEG_INF = -1e30

_IN_QKV = 2 * DN_QK_W + DN_V_W
_IN_Z = _IN_QKV + DN_V_W
_IN_B = _IN_Z + DN_HEADS
_IN_A = _IN_B + DN_HEADS
Q_OFF, K_OFF, V_OFF, Z_OFF = 0, DN_QK_W, 2 * DN_QK_W, _IN_QKV
SW_W = SWA_Q_W + 2 * SWA_KV_W

LANES = 128
SUBLANES = 8
VMEM_BYTES_V7X = 64 * 1024 * 1024
MIB = 1024 * 1024

CHUNK = 128


def _cparams(semantics, vmem_mib):
    assert vmem_mib * MIB < VMEM_BYTES_V7X, vmem_mib
    return pltpu.CompilerParams(dimension_semantics=semantics, vmem_limit_bytes=vmem_mib * MIB)


def _pick(n, pref):
    t = min(n, pref)
    while n % t:
        t //= 2
    return t


def _resident(block_shape, index_map):
    return pl.BlockSpec(block_shape, index_map, pipeline_mode=pl.Buffered(1))


def _mm_ws_body(*refs, n_parts, part_k):
    a_refs = refs[:n_parts]
    w_ref, o_ref, wb_ref = refs[n_parts:]

    @pl.when(pl.program_id(1) == 0)
    def _():
        wb_ref[...] = w_ref[...].astype(BF16)

    acc = jnp.dot(a_refs[0][...], wb_ref[0:part_k, :], preferred_element_type=F32)
    for p in range(1, n_parts):
        acc = acc + jnp.dot(a_refs[p][...], wb_ref[p * part_k:(p + 1) * part_k, :],
                            preferred_element_type=F32)
    o_ref[...] = acc.astype(o_ref.dtype)


def matmul_ws(a_parts, w, layer, out_dtype, *, n_cols=None, bm=512, bn=1024, name="mm"):
    m, part_k = a_parts[0].shape
    n_parts = len(a_parts)
    k = part_k * n_parts
    assert w.shape[1] == k
    n_cols = w.shape[2] if n_cols is None else n_cols
    bm, bn = _pick(m, bm), _pick(n_cols, bn)
    osz = jnp.dtype(out_dtype).itemsize
    vmem = k * bn * (4 + 2) + 2 * (bm * k * 2 + bm * bn * osz) + 2 * bm * bn * 4
    return pl.pallas_call(
        functools.partial(_mm_ws_body, n_parts=n_parts, part_k=part_k),
        grid=(n_cols // bn, m // bm),
        in_specs=[pl.BlockSpec((bm, part_k), lambda j, i: (i, 0)) for _ in range(n_parts)]
        + [_resident((None, k, bn), lambda j, i: (layer, 0, j))],
        out_specs=pl.BlockSpec((bm, bn), lambda j, i: (i, j)),
        out_shape=jax.ShapeDtypeStruct((m, n_cols), out_dtype),
        scratch_shapes=[pltpu.VMEM((k, bn), BF16)],
        compiler_params=_cparams(("arbitrary", "arbitrary"), vmem // MIB + 6),
        name=name,
    )(*a_parts, w)


def _in_proj_body(a_ref, wt_ref, o_ref, wb_ref):
    @pl.when(pl.program_id(1) == 0)
    def _():
        wb_ref[...] = wt_ref[...].astype(BF16)

    o_ref[...] = lax.dot_general(a_ref[...], wb_ref[...], (((1,), (1,)), ((), ())),
                                 preferred_element_type=F32).astype(o_ref.dtype)


def _in_proj_qkv_body(a_ref, wt_ref, cw_ref, o_ref, wb_ref, ext_ref, *, bm, bn, sub, q_panels,
                      qk_panels):
    j = pl.program_id(0)
    i = pl.program_id(1)
    halo = SUBLANES

    @pl.when(i == 0)
    def _():
        wb_ref[...] = wt_ref[...].astype(BF16)
        ext_ref[0:halo, :] = jnp.zeros((halo, bn), F32)

    w = cw_ref[...]
    q_scale = jnp.where(j < q_panels, HEAD_DIM ** -0.5, 1.0)
    for r0 in range(0, bm, sub):
        acc = lax.dot_general(a_ref[r0:r0 + sub, :], wb_ref[...], (((1,), (1,)), ((), ())),
                              preferred_element_type=F32)
        ext_ref[halo + r0:halo + r0 + sub, :] = acc
        y = acc * w[CONV_K - 1:CONV_K, :]
        for tap in range(CONV_K - 1):
            y = y + ext_ref[pl.ds(halo + r0 - (CONV_K - 1) + tap, sub), :] * w[tap:tap + 1, :]
        y = y * jax.nn.sigmoid(y)
        for h in range(bn // HEAD_DIM):
            hs = slice(h * HEAD_DIM, (h + 1) * HEAD_DIM)
            yh = y[:, hs]
            inv = lax.rsqrt(jnp.sum(yh * yh, -1, keepdims=True) + 1e-6) * q_scale
            o_ref[r0:r0 + sub, hs] = (yh * jnp.where(j < qk_panels, inv, 1.0)).astype(o_ref.dtype)
    ext_ref[0:halo, :] = ext_ref[bm:bm + halo, :]


def in_proj(a, wt, layer, *, row0=0, n_cols=None, conv_w=None, bm=1024, bn=512, name="in_proj"):
    m, k = a.shape
    n_cols = wt.shape[1] if n_cols is None else n_cols
    bm, bn = _pick(m, bm), _pick(n_cols, bn)
    assert row0 % bn == 0
    vmem = 2 * bn * k * 4 + bn * k * 2 + 2 * (bm * k * 2 + bm * bn * 2) + bm * bn * 4
    a_spec = pl.BlockSpec((bm, k), lambda j, i: (i, 0))
    w_spec = pl.BlockSpec((None, bn, k), lambda j, i: (layer, row0 // bn + j, 0))
    common = dict(
        grid=(n_cols // bn, m // bm),
        out_specs=pl.BlockSpec((bm, bn), lambda j, i: (i, j)),
        out_shape=jax.ShapeDtypeStruct((m, n_cols), BF16),
        name=name)
    if conv_w is None:
        return pl.pallas_call(
            _in_proj_body, in_specs=[a_spec, w_spec],
            scratch_shapes=[pltpu.VMEM((bn, k), BF16)],
            compiler_params=_cparams(("arbitrary", "arbitrary"), vmem // MIB + 6), **common)(a, wt)
    assert n_cols == _IN_QKV and DN_QK_W % bn == 0 and bn % HEAD_DIM == 0
    vmem += (SUBLANES + bm) * bn * 4 + 2 * bm * bn * 4
    return pl.pallas_call(
        functools.partial(_in_proj_qkv_body, bm=bm, bn=bn, sub=_pick(bm, 256), q_panels=DN_QK_W // bn,
                          qk_panels=2 * DN_QK_W // bn),
        in_specs=[a_spec, w_spec, pl.BlockSpec((None, CONV_K, bn), lambda j, i: (layer, 0, j))],
        scratch_shapes=[pltpu.VMEM((bn, k), BF16), pltpu.VMEM((SUBLANES + bm, bn), F32)],
        compiler_params=_cparams(("arbitrary", "arbitrary"), vmem // MIB + 6), **common)(a, wt, conv_w)


def _layer_norm_rows(r, g, b):
    mu = jnp.mean(r, -1, keepdims=True)
    c = r - mu
    var = jnp.mean(c * c, -1, keepdims=True)
    return c * lax.rsqrt(var + LN_EPS) * g + b


def _add_ln_body(x_ref, y_ref, g_ref, b_ref, of_ref, ob_ref, *, alpha):
    r = alpha * x_ref[...] + y_ref[...].astype(F32)
    o = _layer_norm_rows(r, g_ref[...], b_ref[...])
    of_ref[...] = o
    ob_ref[...] = o.astype(BF16)


def add_layer_norm(x, y, g, b, alpha, *, bm=256):
    t, d = x.shape
    bm = _pick(t, bm)
    row = pl.BlockSpec((bm, d), lambda i: (i, 0))
    vec = pl.BlockSpec((1, d), lambda i: (0, 0))
    vmem_mib = (2 * bm * d * (4 + y.dtype.itemsize + 4 + 2)) // MIB + 8
    return pl.pallas_call(
        functools.partial(_add_ln_body, alpha=alpha),
        grid=(t // bm,),
        in_specs=[row, row, vec, vec],
        out_specs=[row, row],
        out_shape=[jax.ShapeDtypeStruct((t, d), F32), jax.ShapeDtypeStruct((t, d), BF16)],
        compiler_params=_cparams(("parallel",), vmem_mib),
        name="add_ln",
    )(x, y, g.reshape(1, d), b.reshape(1, d))


def _swiglu_up_body(x_ref, w1_ref, w3_ref, h_ref, w1b_ref, w3b_ref):
    @pl.when(pl.program_id(1) == 0)
    def _():
        w1b_ref[...] = w1_ref[...].astype(BF16)
        w3b_ref[...] = w3_ref[...].astype(BF16)

    x = x_ref[...]
    a = jnp.dot(x, w1b_ref[...], preferred_element_type=F32)
    b = jnp.dot(x, w3b_ref[...], preferred_element_type=F32)
    h_ref[...] = (a * jax.nn.sigmoid(a) * b).astype(h_ref.dtype)


def swiglu_up(x, w1, w3, layer, *, bm=1024, bn=512):
    t, d = x.shape
    f = w1.shape[2]
    bm, bn = _pick(t, bm), _pick(f, bn)
    wspec = _resident((None, d, bn), lambda j, i: (layer, 0, j))
    vmem = 2 * d * bn * (4 + 2) + 2 * (bm * d * 2 + bm * bn * 2) + 4 * bm * bn * 4
    return pl.pallas_call(
        _swiglu_up_body,
        grid=(f // bn, t // bm),
        in_specs=[pl.BlockSpec((bm, d), lambda j, i: (i, 0)), wspec, wspec],
        out_specs=pl.BlockSpec((bm, bn), lambda j, i: (i, j)),
        out_shape=jax.ShapeDtypeStruct((t, f), BF16),
        scratch_shapes=[pltpu.VMEM((d, bn), BF16), pltpu.VMEM((d, bn), BF16)],
        compiler_params=_cparams(("arbitrary", "arbitrary"), vmem // MIB + 6),
        name="swiglu_up",
    )(x, w1, w3)


def _softplus(x):
    return jnp.maximum(x, 0.0) + jnp.log(1.0 + jnp.exp(-jnp.abs(x)))


def _dn_gate_body(x_ref, wbt_ref, wat_ref, alog_r_ref, dtb_r_ref, alog_c_ref, dtb_c_ref,
                  beta_ref, gam_ref, gamt_ref, *, tb):
    x = x_ref[...]
    nt = (((1,), (1,)), ((), ()))
    wbt = wbt_ref[...].astype(BF16)
    wat = wat_ref[...].astype(BF16)
    beta_ref[...] = jax.nn.sigmoid(lax.dot_general(x, wbt, nt, preferred_element_type=F32))
    pa = lax.dot_general(x, wat, nt, preferred_element_type=F32)
    g = -jnp.exp(alog_r_ref[...]) * _softplus(pa + dtb_r_ref[...])
    pt = lax.dot_general(wat[:DN_HEADS], x, nt, preferred_element_type=F32)
    gt = -jnp.exp(alog_c_ref[...]) * _softplus(pt + dtb_c_ref[...])
    r = lax.broadcasted_iota(I32, (tb, tb), 0)
    c = lax.broadcasted_iota(I32, (tb, tb), 1)
    same = (r // CHUNK) == (c // CHUNK)
    lower = jnp.where(same & (r >= c), 1.0, 0.0).astype(F32)
    upper = jnp.where(same & (r <= c), 1.0, 0.0).astype(F32)
    gam_ref[...] = jnp.dot(lower, g, precision=lax.Precision.HIGHEST, preferred_element_type=F32)
    gamt_ref[...] = jnp.dot(gt, upper, precision=lax.Precision.HIGHEST, preferred_element_type=F32)


def dn_gates(x, wbt, wat, a_log, dt_bias, *, tb=512):
    t, d = x.shape
    tb = _pick(t, tb)
    pad = lambda v: jnp.zeros((1, LANES), F32).at[0, :DN_HEADS].set(v.astype(F32))
    col = lambda v: v.astype(F32).reshape(DN_HEADS, 1)
    full = lambda shape: pl.BlockSpec(shape, lambda i: (0, 0))
    return pl.pallas_call(
        functools.partial(_dn_gate_body, tb=tb),
        grid=(t // tb,),
        in_specs=[pl.BlockSpec((tb, d), lambda i: (i, 0)),
                  full((LANES, d)), full((LANES, d)),
                  full((1, LANES)), full((1, LANES)), full((DN_HEADS, 1)), full((DN_HEADS, 1))],
        out_specs=[pl.BlockSpec((tb, LANES), lambda i: (i, 0)),
                   pl.BlockSpec((tb, LANES), lambda i: (i, 0)),
                   pl.BlockSpec((DN_HEADS, tb), lambda i: (0, i))],
        out_shape=[jax.ShapeDtypeStruct((t, LANES), F32),
                   jax.ShapeDtypeStruct((t, LANES), F32),
                   jax.ShapeDtypeStruct((DN_HEADS, t), F32)],
        compiler_params=_cparams(("parallel",), 32),
        name="dn_gates",
    )(x, wbt, wat, pad(a_log), pad(dt_bias), col(a_log), col(dt_bias))


def _mm16(a, b):
    return jnp.dot(a.astype(BF16), b.astype(BF16), preferred_element_type=F32)


def _mm16_each(xs, ys):
    return [_mm16(x, y) for x, y in zip(xs, ys)]


def _unit_lower_inverse_each(a_list, eye, diag_blocks):
    d = [jnp.where(diag_blocks, a, 0.0) for a in a_list]
    n = [a - di for a, di in zip(a_list, d)]
    d2 = _mm16_each(d, d)
    x = _mm16_each([eye - t for t in d], [eye + t for t in d2])
    d4 = _mm16_each(d2, d2)
    x = _mm16_each(x, [eye + t for t in d4])
    d8 = _mm16_each(d4, d4)
    x = _mm16_each(x, [eye + t for t in d8])
    m = _mm16_each(x, n)
    m2 = _mm16_each(m, m)
    y = _mm16_each([eye - t for t in m], [eye + t for t in m2])
    m4 = _mm16_each(m2, m2)
    y = _mm16_each(y, [eye + t for t in m4])
    return _mm16_each(y, x)


def _delta_body(q_ref, k_ref, v_ref, z_ref, beta_ref, gam_ref, gamt_ref, nw_ref, o_ref, s_ref, *,
                hb, tb):
    hg = pl.program_id(0)
    t = pl.program_id(1)

    @pl.when(t == 0)
    def _():
        s_ref[...] = jnp.zeros_like(s_ref)

    row = lax.broadcasted_iota(I32, (CHUNK, CHUNK), 0)
    col = lax.broadcasted_iota(I32, (CHUNK, CHUNK), 1)
    causal = row >= col
    strict = row > col
    diag_blocks = (row // 16) == (col // 16)
    eye = jnp.where(row == col, 1.0, 0.0).astype(F32)
    lane = lax.broadcasted_iota(I32, (tb, LANES), 1)
    nw = nw_ref[...]
    heads = range(hb)
    sq = (CHUNK, CHUNK)

    bcol, gcol, grow = [], [], []
    for hl in heads:
        head = hg * hb + hl
        sel = lane == head
        bcol.append(jnp.sum(jnp.where(sel, beta_ref[...], 0.0), axis=-1, keepdims=True))
        gcol.append(jnp.sum(jnp.where(sel, gam_ref[...], 0.0), axis=-1, keepdims=True))
        grow.append(gamt_ref[pl.ds(head, 1), :])

    for c in range(tb // CHUNK):
        rs = slice(c * CHUNK, (c + 1) * CHUNK)
        cs = [slice(hl * HEAD_DIM, (hl + 1) * HEAD_DIM) for hl in heads]
        q = [q_ref[rs, cs[h]].astype(F32) for h in heads]
        k = [k_ref[rs, cs[h]].astype(F32) for h in heads]
        v = [v_ref[rs, cs[h]].astype(F32) for h in heads]
        bc = [jnp.broadcast_to(bcol[h][rs, :], sq) for h in heads]
        gc = [jnp.broadcast_to(gcol[h][rs, :], sq) for h in heads]
        gr = [jnp.broadcast_to(grow[h][:, rs], sq) for h in heads]
        g_last = [jnp.broadcast_to(grow[h][:, (c + 1) * CHUNK - 1:(c + 1) * CHUNK], sq) for h in heads]
        decay = [jnp.exp(jnp.where(causal, gc[h] - gr[h], NEG_INF)) for h in heads]
        egc = [jnp.exp(gc[h]) for h in heads]
        kt = [k[h].T for h in heads]
        gram = _mm16_each([jnp.concatenate([q[h], k[h]], axis=0) for h in heads], kt)
        a_qk = [gram[h][:CHUNK] * decay[h] for h in heads]
        a_kk = [jnp.where(strict, bc[h] * gram[h][CHUNK:] * decay[h], 0.0) for h in heads]
        tinv = _unit_lower_inverse_each(a_kk, eye, diag_blocks)
        sol = _mm16_each(tinv, [jnp.concatenate([v[h] * bc[h], k[h] * (bc[h] * egc[h])], axis=1)
                                for h in heads])
        lhs1 = [jnp.concatenate([sol[h][:, HEAD_DIM:], q[h] * egc[h]], axis=0) for h in heads]
        kdt = [kt[h] * jnp.exp(g_last[h] - gr[h]) for h in heads]
        s = [s_ref[h] for h in heads]
        ws_qs = _mm16_each(lhs1, s)
        v_new = [sol[h][:, :HEAD_DIM] - ws_qs[h][:CHUNK] for h in heads]
        av_ds = _mm16_each([jnp.concatenate([a_qk[h], kdt[h]], axis=0) for h in heads], v_new)
        for h in heads:
            s_ref[h] = s[h] * jnp.exp(g_last[h]) + av_ds[h][CHUNK:]
            o = ws_qs[h][CHUNK:] + av_ds[h][:CHUNK]
            o = o * lax.rsqrt(jnp.mean(o * o, -1, keepdims=True) + 1e-6) * nw
            zz = z_ref[rs, cs[h]].astype(F32)
            o_ref[rs, cs[h]] = (o * (zz * jax.nn.sigmoid(zz))).astype(o_ref.dtype)


def delta_net(qkv, z, beta, gam, gamt, norm_w, *, hb=8, tb=256):
    t = qkv.shape[0]
    tb = _pick(t, tb)
    w = hb * HEAD_DIM
    pspec = lambda off: pl.BlockSpec((tb, w), lambda h, i: (i, off // w + h))
    return pl.pallas_call(
        functools.partial(_delta_body, hb=hb, tb=tb),
        grid=(DN_HEADS // hb, t // tb),
        in_specs=[pspec(Q_OFF), pspec(K_OFF), pspec(V_OFF), pspec(0),
                  pl.BlockSpec((tb, LANES), lambda h, i: (i, 0)),
                  pl.BlockSpec((tb, LANES), lambda h, i: (i, 0)),
                  pl.BlockSpec((DN_HEADS, tb), lambda h, i: (0, i)),
                  pl.BlockSpec((1, HEAD_DIM), lambda h, i: (0, 0))],
        out_specs=pl.BlockSpec((tb, w), lambda h, i: (i, h)),
        out_shape=jax.ShapeDtypeStruct((t, DN_V_W), BF16),
        scratch_shapes=[pltpu.VMEM((hb, HEAD_DIM, HEAD_DIM), F32)],
        compiler_params=_cparams(("arbitrary", "arbitrary"), 48),
        name="delta_net",
    )(qkv, qkv, qkv, z, beta, gam, gamt, norm_w.astype(F32).reshape(1, HEAD_DIM))


def _swa_body(sink_ref, q_ref, kc_ref, kp_ref, vc_ref, vp_ref, o_ref):
    i = pl.program_id(0)
    qi = lax.broadcasted_iota(I32, (WINDOW, 2 * WINDOW), 0)
    kj = lax.broadcasted_iota(I32, (WINDOW, 2 * WINDOW), 1)
    dist = qi - kj + WINDOW
    first_key = jnp.where(i > 0, 0, WINDOW)
    valid = (dist >= 0) & (dist < WINDOW) & (kj >= first_key)
    distf = dist.astype(F32)
    scale = HEAD_DIM ** -0.5
    for h in range(SWA_KV_HEADS):
        hs = slice(h * HEAD_DIM, (h + 1) * HEAD_DIM)
        kk = jnp.concatenate([kp_ref[:, hs], kc_ref[:, hs]], axis=0)
        vv = jnp.concatenate([vp_ref[:, hs], vc_ref[:, hs]], axis=0)
        vv1 = jnp.concatenate([vv, jnp.ones_like(vv)], axis=1)
        for g in range(SWA_GROUP):
            hq = h * SWA_GROUP + g
            slope = 2.0 ** (-8.0 * (hq + 1) / SWA_Q_HEADS)
            qs = slice(hq * HEAD_DIM, (hq + 1) * HEAD_DIM)
            s = lax.dot_general(q_ref[:, qs], kk, (((1,), (1,)), ((), ())),
                                preferred_element_type=F32) * scale
            logits = jnp.where(valid, s - slope * distf, NEG_INF)
            sink = sink_ref[hq]
            m = jnp.maximum(jnp.max(logits, -1, keepdims=True), sink)
            e = jnp.exp(logits - m).astype(BF16)
            ov = jnp.dot(e, vv1, preferred_element_type=F32)
            denom = ov[:, HEAD_DIM:] + jnp.exp(sink - m)
            o_ref[:, qs] = (ov[:, :HEAD_DIM] / denom).astype(o_ref.dtype)


def swa_attention(proj_sw, sinks):
    t = proj_sw.shape[0]
    nb = t // WINDOW
    kb, vb = SWA_Q_W // SWA_KV_W, SWA_Q_W // SWA_KV_W + 1
    cur = lambda blk: pl.BlockSpec((WINDOW, SWA_KV_W), lambda i: (i, blk))
    prev = lambda blk: pl.BlockSpec((WINDOW, SWA_KV_W), lambda i: (jnp.maximum(i - 1, 0), blk))
    return pl.pallas_call(
        _swa_body,
        grid=(nb,),
        in_specs=[pl.BlockSpec(memory_space=pltpu.SMEM),
                  pl.BlockSpec((WINDOW, SWA_Q_W), lambda i: (i, 0)),
                  cur(kb), prev(kb), cur(vb), prev(vb)],
        out_specs=pl.BlockSpec((WINDOW, SWA_Q_W), lambda i: (i, 0)),
        out_shape=jax.ShapeDtypeStruct((t, SWA_Q_W), BF16),
        compiler_params=_cparams(("parallel",), 32),
        name="swa",
    )(sinks.astype(F32), proj_sw, proj_sw, proj_sw, proj_sw, proj_sw)


def _top2_gates(x, w):
    logits = jnp.dot(x, w, precision=lax.Precision.HIGHEST, preferred_element_type=F32)
    lane = lax.broadcasted_iota(I32, logits.shape, 1)
    neg = jnp.float32(-jnp.inf)
    l1 = jnp.where(lane < N_EXPERTS, logits, neg)
    m1 = jnp.max(l1, -1, keepdims=True)
    i1 = jnp.min(jnp.where(l1 == m1, lane, LANES), -1, keepdims=True)
    l2 = jnp.where(lane == i1, neg, l1)
    m2 = jnp.max(l2, -1, keepdims=True)
    i2 = jnp.min(jnp.where(l2 == m2, lane, LANES), -1, keepdims=True)
    e = jnp.exp(m2 - m1)
    w1 = 1.0 / (1.0 + e)
    w2 = e * w1
    idx = jnp.where(lane == 0, i1, jnp.where(lane == 1, i2, 0))
    gates = jnp.where(lane == 0, w1, jnp.where(lane == 1, w2, 0.0))
    return idx, gates


def _add_ln_router_body(x_ref, y_ref, g_ref, b_ref, rw_ref, of_ref, idx_ref, gate_ref, *, alpha):
    r = alpha * x_ref[...] + y_ref[...].astype(F32)
    o = _layer_norm_rows(r, g_ref[...], b_ref[...])
    of_ref[...] = o
    idx_ref[...], gate_ref[...] = _top2_gates(o, rw_ref[...])


def add_layer_norm_router(x, y, g, b, alpha, router_w, *, bm=256):
    t, d = x.shape
    bm = _pick(t, bm)
    w = jnp.zeros((d, LANES), F32).at[:, :N_EXPERTS].set(router_w.astype(F32))
    row = lambda width: pl.BlockSpec((bm, width), lambda i: (i, 0))
    vec = pl.BlockSpec((1, d), lambda i: (0, 0))
    vmem_mib = (2 * bm * d * (4 + y.dtype.itemsize + 4) + 2 * d * LANES * 4 + 6 * bm * d * 4) // MIB + 8
    return pl.pallas_call(
        functools.partial(_add_ln_router_body, alpha=alpha),
        grid=(t // bm,),
        in_specs=[row(d), row(d), vec, vec, pl.BlockSpec((d, LANES), lambda i: (0, 0))],
        out_specs=[row(d), row(LANES), row(LANES)],
        out_shape=[jax.ShapeDtypeStruct((t, d), F32),
                   jax.ShapeDtypeStruct((t, LANES), I32), jax.ShapeDtypeStruct((t, LANES), F32)],
        compiler_params=_cparams(("parallel",), vmem_mib),
        name="add_ln_router",
    )(x, y, g.reshape(1, d), b.reshape(1, d), w)


def _route_plan(top_i, tm):
    t = top_i.shape[0]
    n_assign = 2 * t
    e_flat = top_i.reshape(-1)
    onehot = (e_flat[:, None] == jnp.arange(N_EXPERTS, dtype=I32)[None, :]).astype(I32)
    csum = jnp.cumsum(onehot, axis=0)
    rank = jnp.sum(csum * onehot, axis=1) - 1
    counts = csum[-1]
    padded = ((counts + tm - 1) // tm) * tm
    ends = jnp.cumsum(padded)
    offs = ends - padded
    pos = offs[e_flat] + rank
    n_tiles = n_assign // tm + N_EXPERTS
    row_src = jnp.zeros((n_tiles * tm,), I32).at[pos].set(jnp.arange(n_assign, dtype=I32) // 2)
    tile_start = jnp.arange(n_tiles, dtype=I32) * tm
    tile_e = jnp.sum((tile_start[:, None] >= ends[None, :]).astype(I32), axis=1)
    tile_valid = (tile_start < ends[-1]).astype(I32)
    last_e = jnp.max(jnp.where(padded > 0, jnp.arange(N_EXPERTS, dtype=I32), 0))
    tile_e = jnp.minimum(tile_e, last_e)
    return row_src, pos[0::2], pos[1::2], tile_e, tile_valid, n_tiles


_ISSUE_UNROLL = 8


def _gather_rows_body(src_ref, x_hbm, o_ref, buf, sem, *, rows):
    i = pl.program_id(0)
    slot = lax.rem(i, 2)

    def row_copy(step_slot, r, src_row):
        return pltpu.make_async_copy(x_hbm.at[pl.ds(src_row, 1), :],
                                     buf.at[step_slot, pl.ds(r, 1), :], sem.at[step_slot])

    def issue(step, step_slot):
        def body(r, carry):
            row_copy(step_slot, r, src_ref[step * rows + r]).start()
            return carry
        lax.fori_loop(0, rows, body, 0, unroll=_ISSUE_UNROLL)

    @pl.when(i == 0)
    def _():
        issue(0, 0)

    @pl.when(i + 1 < pl.num_programs(0))
    def _():
        issue(i + 1, 1 - slot)

    def drain(r, carry):
        row_copy(slot, r, 0).wait()
        return carry

    lax.fori_loop(0, rows, drain, 0, unroll=_ISSUE_UNROLL)
    o_ref[...] = buf[slot].astype(o_ref.dtype)


def gather_rows(x, row_src, *, rows=512):
    d = x.shape[1]
    p = row_src.shape[0]
    rows = _pick(p, rows)
    return pl.pallas_call(
        functools.partial(_gather_rows_body, rows=rows),
        grid_spec=pltpu.PrefetchScalarGridSpec(
            num_scalar_prefetch=1,
            grid=(p // rows,),
            in_specs=[pl.BlockSpec(memory_space=pl.ANY)],
            out_specs=pl.BlockSpec((rows, d), lambda i, src: (i, 0)),
            scratch_shapes=[pltpu.VMEM((2, rows, d), F32), pltpu.SemaphoreType.DMA((2,))]),
        out_shape=jax.ShapeDtypeStruct((p, d), BF16),
        compiler_params=_cparams(("arbitrary",), (rows * d * (2 * 4 + 2 * 2 + 4)) // MIB + 6),
        name="moe_gather",
    )(row_src, x)


def _new_panel(te_ref, i):
    return (i == 0) | (te_ref[i] != te_ref[jnp.maximum(i - 1, 0)])


def _expert_up_body(te_ref, tv_ref, x_ref, w1_ref, w3_ref, h_ref, w1b_ref, w3b_ref):
    i = pl.program_id(1)

    @pl.when(_new_panel(te_ref, i))
    def _():
        w1b_ref[...] = w1_ref[...].astype(BF16)
        w3b_ref[...] = w3_ref[...].astype(BF16)

    @pl.when(tv_ref[i] == 1)
    def _():
        x = x_ref[...]
        a = jnp.dot(x, w1b_ref[...], preferred_element_type=F32)
        b = jnp.dot(x, w3b_ref[...], preferred_element_type=F32)
        h_ref[...] = (a * jax.nn.sigmoid(a) * b).astype(h_ref.dtype)

    @pl.when(tv_ref[i] == 0)
    def _():
        h_ref[...] = jnp.zeros_like(h_ref)


def expert_up(xs, w1, w3, layer, tile_e, tile_valid, tm, *, bn=512):
    p, d = xs.shape
    f = w1.shape[3]
    bn = _pick(f, bn)
    wspec = _resident((None, None, d, bn), lambda j, i, te, tv: (layer, te[i], 0, j))
    vmem = 2 * d * bn * (4 + 2) + 2 * (tm * d * 2 + tm * bn * 2) + 4 * tm * bn * 4
    return pl.pallas_call(
        _expert_up_body,
        grid_spec=pltpu.PrefetchScalarGridSpec(
            num_scalar_prefetch=2,
            grid=(f // bn, p // tm),
            in_specs=[pl.BlockSpec((tm, d), lambda j, i, te, tv: (i, 0)), wspec, wspec],
            out_specs=pl.BlockSpec((tm, bn), lambda j, i, te, tv: (i, j)),
            scratch_shapes=[pltpu.VMEM((d, bn), BF16), pltpu.VMEM((d, bn), BF16)]),
        out_shape=jax.ShapeDtypeStruct((p, f), BF16),
        compiler_params=_cparams(("arbitrary", "arbitrary"), vmem // MIB + 6),
        name="expert_up",
    )(tile_e, tile_valid, xs, w1, w3)


def _expert_down_body(te_ref, tv_ref, h_ref, w2_ref, y_ref, w2b_ref):
    i = pl.program_id(1)

    @pl.when(_new_panel(te_ref, i))
    def _():
        w2b_ref[...] = w2_ref[...].astype(BF16)

    @pl.when(tv_ref[i] == 1)
    def _():
        y_ref[...] = jnp.dot(h_ref[...], w2b_ref[...], preferred_element_type=F32).astype(y_ref.dtype)

    @pl.when(tv_ref[i] == 0)
    def _():
        y_ref[...] = jnp.zeros_like(y_ref)


def expert_down(h, w2, layer, tile_e, tile_valid, tm, *, bn=1024):
    p, f = h.shape
    d = w2.shape[3]
    bn = _pick(d, bn)
    vmem = f * bn * (2 * 4 + 2) + 2 * (tm * f * 2 + tm * bn * 4) + 2 * tm * bn * 4
    return pl.pallas_call(
        _expert_down_body,
        grid_spec=pltpu.PrefetchScalarGridSpec(
            num_scalar_prefetch=2,
            grid=(d // bn, p // tm),
            in_specs=[pl.BlockSpec((tm, f), lambda j, i, te, tv: (i, 0)),
                      pl.BlockSpec((None, None, f, bn), lambda j, i, te, tv: (layer, te[i], 0, j))],
            out_specs=pl.BlockSpec((tm, bn), lambda j, i, te, tv: (i, j)),
            scratch_shapes=[pltpu.VMEM((f, bn), BF16)]),
        out_shape=jax.ShapeDtypeStruct((p, d), F32),
        compiler_params=_cparams(("arbitrary", "arbitrary"), vmem // MIB + 6),
        name="expert_down",
    )(tile_e, tile_valid, h, w2)


def _combine_ln_body(p1_ref, p2_ref, y_hbm, x_ref, gate_ref, g_ref, b_ref, of_ref, ob_ref,
                     buf, sem, *, rows, alpha):
    i = pl.program_id(0)
    slot = lax.rem(i, 2)

    def row_copy(step_slot, which, r, src_row):
        return pltpu.make_async_copy(y_hbm.at[pl.ds(src_row, 1), :],
                                     buf.at[step_slot, which, pl.ds(r, 1), :], sem.at[step_slot])

    def issue(step, step_slot):
        def body(r, carry):
            row_copy(step_slot, 0, r, p1_ref[step * rows + r]).start()
            row_copy(step_slot, 1, r, p2_ref[step * rows + r]).start()
            return carry
        lax.fori_loop(0, rows, body, 0, unroll=_ISSUE_UNROLL)

    @pl.when(i == 0)
    def _():
        issue(0, 0)

    @pl.when(i + 1 < pl.num_programs(0))
    def _():
        issue(i + 1, 1 - slot)

    def drain(r, carry):
        row_copy(slot, 0, r, 0).wait()
        row_copy(slot, 1, r, 0).wait()
        return carry

    lax.fori_loop(0, rows, drain, 0, unroll=_ISSUE_UNROLL)
    gates = gate_ref[...]
    f = gates[:, 0:1] * buf[slot, 0] + gates[:, 1:2] * buf[slot, 1]
    o = _layer_norm_rows(alpha * x_ref[...] + f, g_ref[...], b_ref[...])
    of_ref[...] = o
    ob_ref[...] = o.astype(BF16)


def moe_combine_ln(y, pos1, pos2, gates, x, g, b, alpha, *, rows=128):
    t, d = x.shape
    rows = _pick(t, rows)
    row = lambda width: pl.BlockSpec((rows, width), lambda i, p1, p2: (i, 0))
    vec = pl.BlockSpec((1, d), lambda i, p1, p2: (0, 0))
    return pl.pallas_call(
        functools.partial(_combine_ln_body, rows=rows, alpha=alpha),
        grid_spec=pltpu.PrefetchScalarGridSpec(
            num_scalar_prefetch=2,
            grid=(t // rows,),
            in_specs=[pl.BlockSpec(memory_space=pl.ANY), row(d), row(LANES), vec, vec],
            out_specs=[row(d), row(d)],
            scratch_shapes=[pltpu.VMEM((2, 2, rows, d), F32), pltpu.SemaphoreType.DMA((2,))]),
        out_shape=[jax.ShapeDtypeStruct((t, d), F32), jax.ShapeDtypeStruct((t, d), BF16)],
        compiler_params=_cparams(("arbitrary",), (rows * d * (16 + 2 * (4 + 4 + 2) + 8)) // MIB + 6),
        name="moe_combine_ln",
    )(pos1, pos2, y, x, gates, g.reshape(1, d), b.reshape(1, d))


def moe_ffn_ln(x_f32, idx, gates, w1, w3, w2, layer, g, b, alpha, *, tm=512):
    row_src, pos1, pos2, tile_e, tile_valid, _ = _route_plan(idx[:, :2], tm)
    xs = gather_rows(x_f32, row_src, rows=tm)
    h = expert_up(xs, w1, w3, layer, tile_e, tile_valid, tm)
    y = expert_down(h, w2, layer, tile_e, tile_valid, tm)
    return moe_combine_ln(y, pos1, pos2, gates, x_f32, g, b, alpha)


def _gate_weights(wt_in, layer):
    d = wt_in.shape[2]
    pad = lambda rows: jnp.zeros((LANES, d), F32).at[:DN_HEADS].set(rows)
    return pad(wt_in[layer, _IN_Z:_IN_B, :]), pad(wt_in[layer, _IN_B:_IN_A, :])


def kernel(x, w_in, conv_w, a_log, dt_bias, dn_norm_w, sinks, w_o, ln1_g, ln1_b, ffn_w1, ffn_w3,
           ffn_w2, router_w, exp_w1, exp_w3, exp_w2, ln2_g, ln2_b):
    bsz, seq, d = x.shape
    depth = w_in.shape[0]
    alpha = (2 * depth) ** 0.25
    w_in, w_o, conv_w = w_in.astype(F32), w_o.astype(F32), conv_w.astype(F32)
    wt_in = jnp.swapaxes(w_in, 1, 2)
    wt_sw = wt_in[:, _IN_A:, :]
    outs = []
    for bi in range(bsz):
        xf = x[bi].astype(F32)
        xb = xf.astype(BF16)
        for i in range(depth):
            w_bt, w_at = _gate_weights(wt_in, i)
            qkv = in_proj(xb, wt_in, i, n_cols=_IN_QKV, conv_w=conv_w, name="in_proj_qkv")
            z = in_proj(xb, wt_in, i, row0=_IN_QKV, n_cols=DN_V_W, name="in_proj_z")
            proj_sw = in_proj(xb, wt_sw, i, name="in_proj_sw")
            beta, gam, gamt = dn_gates(xb, w_bt, w_at, a_log[i], dt_bias[i])
            out_a = delta_net(qkv, z, beta, gam, gamt, dn_norm_w[i])
            out_b = swa_attention(proj_sw, sinks[i])
            mix = matmul_ws([out_a, out_b], w_o, i, BF16, name="out_proj")
            j = i // 2
            if i % 2 == 0:
                xf, xb = add_layer_norm(xf, mix, ln1_g[i], ln1_b[i], alpha)
                h = swiglu_up(xb, ffn_w1.astype(F32), ffn_w3.astype(F32), j)
                f = matmul_ws([h], ffn_w2.astype(F32), j, BF16, bm=512, bn=512, name="ffn_down")
                xf, xb = add_layer_norm(xf, f, ln2_g[i], ln2_b[i], alpha)
            else:
                xf, idx, gates = add_layer_norm_router(xf, mix, ln1_g[i], ln1_b[i], alpha, router_w[j])
                xf, xb = moe_ffn_ln(xf, idx, gates, exp_w1.astype(F32), exp_w3.astype(F32),
                                    exp_w2.astype(F32), j, ln2_g[i], ln2_b[i], alpha)
        outs.append(xf)
    return jnp.stack(outs, axis=0).astype(x.dtype)
```

```python
import functools

import jax
import jax.numpy as jnp
from jax import lax
from jax.experimental import pallas as pl
from jax.experimental.pallas import tpu as pltpu

F32 = jnp.float32
BF16 = jnp.bfloat16
I32 = jnp.int32

HEAD_DIM = 128
DN_HEADS = 16
DN_QK_W = DN_HEADS * HEAD_DIM
DN_V_W = DN_HEADS * HEAD_DIM
CONV_K = 4
SWA_Q_HEADS = 16
SWA_KV_HEADS = 4
SWA_GROUP = SWA_Q_HEADS // SWA_KV_HEADS
SWA_Q_W = SWA_Q_HEADS * HEAD_DIM
SWA_KV_W = SWA_KV_HEADS * HEAD_DIM
WINDOW = 128
N_EXPERTS = 8
LN_EPS = 1e-5
NEG_INF = -1e30

_IN_QKV = 2 * DN_QK_W + DN_V_W
_IN_Z = _IN_QKV + DN_V_W
_IN_B = _IN_Z + DN_HEADS
_IN_A = _IN_B + DN_HEADS
Q_OFF, K_OFF, V_OFF, Z_OFF = 0, DN_QK_W, 2 * DN_QK_W, _IN_QKV
SW_W = SWA_Q_W + 2 * SWA_KV_W

LANES = 128
SUBLANES = 8
VMEM_BYTES_V7X = 64 * 1024 * 1024
MIB = 1024 * 1024

CHUNK = 128


def _cparams(semantics, vmem_mib):
    assert vmem_mib * MIB < VMEM_BYTES_V7X, vmem_mib
    return pltpu.CompilerParams(dimension_semantics=semantics, vmem_limit_bytes=vmem_mib * MIB)


def _pick(n, pref):
    t = min(n, pref)
    while n % t:
        t //= 2
    return t


def _resident(block_shape, index_map):
    return pl.BlockSpec(block_shape, index_map, pipeline_mode=pl.Buffered(1))


def _mm_ws_body(*refs, n_parts, part_k):
    a_refs = refs[:n_parts]
    w_ref, o_ref, wb_ref = refs[n_parts:]

    @pl.when(pl.program_id(1) == 0)
    def _():
        wb_ref[...] = w_ref[...].astype(BF16)

    acc = jnp.dot(a_refs[0][...], wb_ref[0:part_k, :], preferred_element_type=F32)
    for p in range(1, n_parts):
        acc = acc + jnp.dot(a_refs[p][...], wb_ref[p * part_k:(p + 1) * part_k, :],
                            preferred_element_type=F32)
    o_ref[...] = acc.astype(o_ref.dtype)


def matmul_ws(a_parts, w, layer, out_dtype, *, n_cols=None, bm=512, bn=1024, name="mm"):
    m, part_k = a_parts[0].shape
    n_parts = len(a_parts)
    k = part_k * n_parts
    assert w.shape[1] == k
    n_cols = w.shape[2] if n_cols is None else n_cols
    bm, bn = _pick(m, bm), _pick(n_cols, bn)
    osz = jnp.dtype(out_dtype).itemsize
    vmem = k * bn * (4 + 2) + 2 * (bm * k * 2 + bm * bn * osz) + 2 * bm * bn * 4
    return pl.pallas_call(
        functools.partial(_mm_ws_body, n_parts=n_parts, part_k=part_k),
        grid=(n_cols // bn, m // bm),
        in_specs=[pl.BlockSpec((bm, part_k), lambda j, i: (i, 0)) for _ in range(n_parts)]
        + [_resident((None, k, bn), lambda j, i: (layer, 0, j))],
        out_specs=pl.BlockSpec((bm, bn), lambda j, i: (i, j)),
        out_shape=jax.ShapeDtypeStruct((m, n_cols), out_dtype),
        scratch_shapes=[pltpu.VMEM((k, bn), BF16)],
        compiler_params=_cparams(("arbitrary", "arbitrary"), vmem // MIB + 6),
        name=name,
    )(*a_parts, w)


def _in_proj_body(a_ref, wt_ref, o_ref, wb_ref):
    @pl.when(pl.program_id(1) == 0)
    def _():
        wb_ref[...] = wt_ref[...].astype(BF16)

    o_ref[...] = lax.dot_general(a_ref[...], wb_ref[...], (((1,), (1,)), ((), ())),
                                 preferred_element_type=F32).astype(o_ref.dtype)


def _in_proj_qkv_body(a_ref, wt_ref, cw_ref, o_ref, wb_ref, ext_ref, *, bm, bn, sub, q_panels,
                      qk_panels):
    j = pl.program_id(0)
    i = pl.program_id(1)
    halo = SUBLANES

    @pl.when(i == 0)
    def _():
        wb_ref[...] = wt_ref[...].astype(BF16)
        ext_ref[0:halo, :] = jnp.zeros((halo, bn), F32)

    w = cw_ref[...]
    q_scale = jnp.where(j < q_panels, HEAD_DIM ** -0.5, 1.0)
    for r0 in range(0, bm, sub):
        acc = lax.dot_general(a_ref[r0:r0 + sub, :], wb_ref[...], (((1,), (1,)), ((), ())),
                              preferred_element_type=F32)
        ext_ref[halo + r0:halo + r0 + sub, :] = acc
        y = acc * w[CONV_K - 1:CONV_K, :]
        for tap in range(CONV_K - 1):
            y = y + ext_ref[pl.ds(halo + r0 - (CONV_K - 1) + tap, sub), :] * w[tap:tap + 1, :]
        y = y * jax.nn.sigmoid(y)
        for h in range(bn // HEAD_DIM):
            hs = slice(h * HEAD_DIM, (h + 1) * HEAD_DIM)
            yh = y[:, hs]
            inv = lax.rsqrt(jnp.sum(yh * yh, -1, keepdims=True) + 1e-6) * q_scale
            o_ref[r0:r0 + sub, hs] = (yh * jnp.where(j < qk_panels, inv, 1.0)).astype(o_ref.dtype)
    ext_ref[0:halo, :] = ext_ref[bm:bm + halo, :]


def in_proj(a, wt, layer, *, row0=0, n_cols=None, conv_w=None, bm=1024, bn=512, name="in_proj"):
    m, k = a.shape
    n_cols = wt.shape[1] if n_cols is None else n_cols
    bm, bn = _pick(m, bm), _pick(n_cols, bn)
    assert row0 % bn == 0
    vmem = 2 * bn * k * 4 + bn * k * 2 + 2 * (bm * k * 2 + bm * bn * 2) + bm * bn * 4
    a_spec = pl.BlockSpec((bm, k), lambda j, i: (i, 0))
    w_spec = pl.BlockSpec((None, bn, k), lambda j, i: (layer, row0 // bn + j, 0))
    common = dict(
        grid=(n_cols // bn, m // bm),
        out_specs=pl.BlockSpec((bm, bn), lambda j, i: (i, j)),
        out_shape=jax.ShapeDtypeStruct((m, n_cols), BF16),
        name=name)
    if conv_w is None:
        return pl.pallas_call(
            _in_proj_body, in_specs=[a_spec, w_spec],
            scratch_shapes=[pltpu.VMEM((bn, k), BF16)],
            compiler_params=_cparams(("arbitrary", "arbitrary"), vmem // MIB + 6), **common)(a, wt)
    assert n_cols == _IN_QKV and DN_QK_W % bn == 0 and bn % HEAD_DIM == 0
    vmem += (SUBLANES + bm) * bn * 4 + 2 * bm * bn * 4
    return pl.pallas_call(
        functools.partial(_in_proj_qkv_body, bm=bm, bn=bn, sub=_pick(bm, 256), q_panels=DN_QK_W // bn,
                          qk_panels=2 * DN_QK_W // bn),
        in_specs=[a_spec, w_spec, pl.BlockSpec((None, CONV_K, bn), lambda j, i: (layer, 0, j))],
        scratch_shapes=[pltpu.VMEM((bn, k), BF16), pltpu.VMEM((SUBLANES + bm, bn), F32)],
        compiler_params=_cparams(("arbitrary", "arbitrary"), vmem // MIB + 6), **common)(a, wt, conv_w)


def _layer_norm_rows(r, g, b):
    mu = jnp.mean(r, -1, keepdims=True)
    c = r - mu
    var = jnp.mean(c * c, -1, keepdims=True)
    return c * lax.rsqrt(var + LN_EPS) * g + b


def _add_ln_body(x_ref, y_ref, g_ref, b_ref, of_ref, ob_ref, *, alpha):
    r = alpha * x_ref[...] + y_ref[...].astype(F32)
    o = _layer_norm_rows(r, g_ref[...], b_ref[...])
    of_ref[...] = o
    ob_ref[...] = o.astype(BF16)


def add_layer_norm(x, y, g, b, alpha, *, bm=256):
    t, d = x.shape
    bm = _pick(t, bm)
    row = pl.BlockSpec((bm, d), lambda i: (i, 0))
    vec = pl.BlockSpec((1, d), lambda i: (0, 0))
    vmem_mib = (2 * bm * d * (4 + y.dtype.itemsize + 4 + 2)) // MIB + 8
    return pl.pallas_call(
        functools.partial(_add_ln_body, alpha=alpha),
        grid=(t // bm,),
        in_specs=[row, row, vec, vec],
        out_specs=[row, row],
        out_shape=[jax.ShapeDtypeStruct((t, d), F32), jax.ShapeDtypeStruct((t, d), BF16)],
        compiler_params=_cparams(("parallel",), vmem_mib),
        name="add_ln",
    )(x, y, g.reshape(1, d), b.reshape(1, d))


def _swiglu_up_body(x_ref, w1_ref, w3_ref, h_ref, w1b_ref, w3b_ref):
    @pl.when(pl.program_id(1) == 0)
    def _():
        w1b_ref[...] = w1_ref[...].astype(BF16)
        w3b_ref[...] = w3_ref[...].astype(BF16)

    x = x_ref[...]
    a = jnp.dot(x, w1b_ref[...], preferred_element_type=F32)
    b = jnp.dot(x, w3b_ref[...], preferred_element_type=F32)
    h_ref[...] = (a * jax.nn.sigmoid(a) * b).astype(h_ref.dtype)


def swiglu_up(x, w1, w3, layer, *, bm=1024, bn=512):
    t, d = x.shape
    f = w1.shape[2]
    bm, bn = _pick(t, bm), _pick(f, bn)
    wspec = _resident((None, d, bn), lambda j, i: (layer, 0, j))
    vmem = 2 * d * bn * (4 + 2) + 2 * (bm * d * 2 + bm * bn * 2) + 4 * bm * bn * 4
    return pl.pallas_call(
        _swiglu_up_body,
        grid=(f // bn, t // bm),
        in_specs=[pl.BlockSpec((bm, d), lambda j, i: (i, 0)), wspec, wspec],
        out_specs=pl.BlockSpec((bm, bn), lambda j, i: (i, j)),
        out_shape=jax.ShapeDtypeStruct((t, f), BF16),
        scratch_shapes=[pltpu.VMEM((d, bn), BF16), pltpu.VMEM((d, bn), BF16)],
        compiler_params=_cparams(("arbitrary", "arbitrary"), vmem // MIB + 6),
        name="swiglu_up",
    )(x, w1, w3)


def _softplus(x):
    return jnp.maximum(x, 0.0) + jnp.log(1.0 + jnp.exp(-jnp.abs(x)))


def _dn_gate_body(x_ref, wbt_ref, wat_ref, alog_r_ref, dtb_r_ref, alog_c_ref, dtb_c_ref,
                  beta_ref, gam_ref, gamt_ref, *, tb):
    x = x_ref[...]
    nt = (((1,), (1,)), ((), ()))
    wbt = wbt_ref[...].astype(BF16)
    wat = wat_ref[...].astype(BF16)
    beta_ref[...] = jax.nn.sigmoid(lax.dot_general(x, wbt, nt, preferred_element_type=F32))
    pa = lax.dot_general(x, wat, nt, preferred_element_type=F32)
    g = -jnp.exp(alog_r_ref[...]) * _softplus(pa + dtb_r_ref[...])
    pt = lax.dot_general(wat[:DN_HEADS], x, nt, preferred_element_type=F32)
    gt = -jnp.exp(alog_c_ref[...]) * _softplus(pt + dtb_c_ref[...])
    r = lax.broadcasted_iota(I32, (tb, tb), 0)
    c = lax.broadcasted_iota(I32, (tb, tb), 1)
    same = (r // CHUNK) == (c // CHUNK)
    lower = jnp.where(same & (r >= c), 1.0, 0.0).astype(F32)
    upper = jnp.where(same & (r <= c), 1.0, 0.0).astype(F32)
    gam_ref[...] = jnp.dot(lower, g, precision=lax.Precision.HIGHEST, preferred_element_type=F32)
    gamt_ref[...] = jnp.dot(gt, upper, precision=lax.Precision.HIGHEST, preferred_element_type=F32)


def dn_gates(x, wbt, wat, a_log, dt_bias, *, tb=512):
    t, d = x.shape
    tb = _pick(t, tb)
    pad = lambda v: jnp.zeros((1, LANES), F32).at[0, :DN_HEADS].set(v.astype(F32))
    col = lambda v: v.astype(F32).reshape(DN_HEADS, 1)
    full = lambda shape: pl.BlockSpec(shape, lambda i: (0, 0))
    return pl.pallas_call(
        functools.partial(_dn_gate_body, tb=tb),
        grid=(t // tb,),
        in_specs=[pl.BlockSpec((tb, d), lambda i: (i, 0)),
                  full((LANES, d)), full((LANES, d)),
                  full((1, LANES)), full((1, LANES)), full((DN_HEADS, 1)), full((DN_HEADS, 1))],
        out_specs=[pl.BlockSpec((tb, LANES), lambda i: (i, 0)),
                   pl.BlockSpec((tb, LANES), lambda i: (i, 0)),
                   pl.BlockSpec((DN_HEADS, tb), lambda i: (0, i))],
        out_shape=[jax.ShapeDtypeStruct((t, LANES), F32),
                   jax.ShapeDtypeStruct((t, LANES), F32),
                   jax.ShapeDtypeStruct((DN_HEADS, t), F32)],
        compiler_params=_cparams(("parallel",), 32),
        name="dn_gates",
    )(x, wbt, wat, pad(a_log), pad(dt_bias), col(a_log), col(dt_bias))


def _mm16(a, b):
    return jnp.dot(a.astype(BF16), b.astype(BF16), preferred_element_type=F32)


def _mm16_each(xs, ys):
    return [_mm16(x, y) for x, y in zip(xs, ys)]


def _unit_lower_inverse_each(a_list, eye, diag_blocks):
    d = [jnp.where(diag_blocks, a, 0.0) for a in a_list]
    n = [a - di for a, di in zip(a_list, d)]
    d2 = _mm16_each(d, d)
    x = _mm16_each([eye - t for t in d], [eye + t for t in d2])
    d4 = _mm16_each(d2, d2)
    x = _mm16_each(x, [eye + t for t in d4])
    d8 = _mm16_each(d4, d4)
    x = _mm16_each(x, [eye + t for t in d8])
    m = _mm16_each(x, n)
    m2 = _mm16_each(m, m)
    y = _mm16_each([eye - t for t in m], [eye + t for t in m2])
    m4 = _mm16_each(m2, m2)
    y = _mm16_each(y, [eye + t for t in m4])
    return _mm16_each(y, x)


def _delta_body(q_ref, k_ref, v_ref, z_ref, beta_ref, gam_ref, gamt_ref, nw_ref, o_ref, s_ref, *,
                hb, tb):
    hg = pl.program_id(0)
    t = pl.program_id(1)

    @pl.when(t == 0)
    def _():
        s_ref[...] = jnp.zeros_like(s_ref)

    row = lax.broadcasted_iota(I32, (CHUNK, CHUNK), 0)
    col = lax.broadcasted_iota(I32, (CHUNK, CHUNK), 1)
    causal = row >= col
    strict = row > col
    diag_blocks = (row // 16) == (col // 16)
    eye = jnp.where(row == col, 1.0, 0.0).astype(F32)
    lane = lax.broadcasted_iota(I32, (tb, LANES), 1)
    nw = nw_ref[...]
    heads = range(hb)
    sq = (CHUNK, CHUNK)

    bcol, gcol, grow = [], [], []
    for hl in heads:
        head = hg * hb + hl
        sel = lane == head
        bcol.append(jnp.sum(jnp.where(sel, beta_ref[...], 0.0), axis=-1, keepdims=True))
        gcol.append(jnp.sum(jnp.where(sel, gam_ref[...], 0.0), axis=-1, keepdims=True))
        grow.append(gamt_ref[pl.ds(head, 1), :])

    for c in range(tb // CHUNK):
        rs = slice(c * CHUNK, (c + 1) * CHUNK)
        cs = [slice(hl * HEAD_DIM, (hl + 1) * HEAD_DIM) for hl in heads]
        q = [q_ref[rs, cs[h]].astype(F32) for h in heads]
        k = [k_ref[rs, cs[h]].astype(F32) for h in heads]
        v = [v_ref[rs, cs[h]].astype(F32) for h in heads]
        bc = [jnp.broadcast_to(bcol[h][rs, :], sq) for h in heads]
        gc = [jnp.broadcast_to(gcol[h][rs, :], sq) for h in heads]
        gr = [jnp.broadcast_to(grow[h][:, rs], sq) for h in heads]
        g_last = [jnp.broadcast_to(grow[h][:, (c + 1) * CHUNK - 1:(c + 1) * CHUNK], sq) for h in heads]
        decay = [jnp.exp(jnp.where(causal, gc[h] - gr[h], NEG_INF)) for h in heads]
        egc = [jnp.exp(gc[h]) for h in heads]
        kt = [k[h].T for h in heads]
        gram = _mm16_each([jnp.concatenate([q[h], k[h]], axis=0) for h in heads], kt)
        a_qk = [gram[h][:CHUNK] * decay[h] for h in heads]
        a_kk = [jnp.where(strict, bc[h] * gram[h][CHUNK:] * decay[h], 0.0) for h in heads]
        tinv = _unit_lower_inverse_each(a_kk, eye, diag_blocks)
        sol = _mm16_each(tinv, [jnp.concatenate([v[h] * bc[h], k[h] * (bc[h] * egc[h])], axis=1)
                                for h in heads])
        lhs1 = [jnp.concatenate([sol[h][:, HEAD_DIM:], q[h] * egc[h]], axis=0) for h in heads]
        kdt = [kt[h] * jnp.exp(g_last[h] - gr[h]) for h in heads]
        s = [s_ref[h] for h in heads]
        ws_qs = _mm16_each(lhs1, s)
        v_new = [sol[h][:, :HEAD_DIM] - ws_qs[h][:CHUNK] for h in heads]
        av_ds = _mm16_each([jnp.concatenate([a_qk[h], kdt[h]], axis=0) for h in heads], v_new)
        for h in heads:
            s_ref[h] = s[h] * jnp.exp(g_last[h]) + av_ds[h][CHUNK:]
            o = ws_qs[h][CHUNK:] + av_ds[h][:CHUNK]
            o = o * lax.rsqrt(jnp.mean(o * o, -1, keepdims=True) + 1e-6) * nw
            zz = z_ref[rs, cs[h]].astype(F32)
            o_ref[rs, cs[h]] = (o * (zz * jax.nn.sigmoid(zz))).astype(o_ref.dtype)


def delta_net(qkv, z, beta, gam, gamt, norm_w, *, z_off=0, hb=8, tb=256):
    t = qkv.shape[0]
    tb = _pick(t, tb)
    w = hb * HEAD_DIM
    pspec = lambda off: pl.BlockSpec((tb, w), lambda h, i: (i, off // w + h))
    return pl.pallas_call(
        functools.partial(_delta_body, hb=hb, tb=tb),
        grid=(DN_HEADS // hb, t // tb),
        in_specs=[pspec(Q_OFF), pspec(K_OFF), pspec(V_OFF), pspec(z_off),
                  pl.BlockSpec((tb, LANES), lambda h, i: (i, 0)),
                  pl.BlockSpec((tb, LANES), lambda h, i: (i, 0)),
                  pl.BlockSpec((DN_HEADS, tb), lambda h, i: (0, i)),
                  pl.BlockSpec((1, HEAD_DIM), lambda h, i: (0, 0))],
        out_specs=pl.BlockSpec((tb, w), lambda h, i: (i, h)),
        out_shape=jax.ShapeDtypeStruct((t, DN_V_W), BF16),
        scratch_shapes=[pltpu.VMEM((hb, HEAD_DIM, HEAD_DIM), F32)],
        compiler_params=_cparams(("arbitrary", "arbitrary"), 48),
        name="delta_net",
    )(qkv, qkv, qkv, z, beta, gam, gamt, norm_w.astype(F32).reshape(1, HEAD_DIM))


def _swa_body(sink_ref, q_ref, kc_ref, kp_ref, vc_ref, vp_ref, o_ref):
    i = pl.program_id(0)
    qi = lax.broadcasted_iota(I32, (WINDOW, 2 * WINDOW), 0)
    kj = lax.broadcasted_iota(I32, (WINDOW, 2 * WINDOW), 1)
    dist = qi - kj + WINDOW
    first_key = jnp.where(i > 0, 0, WINDOW)
    valid = (dist >= 0) & (dist < WINDOW) & (kj >= first_key)
    distf = dist.astype(F32)
    scale = HEAD_DIM ** -0.5
    for h in range(SWA_KV_HEADS):
        hs = slice(h * HEAD_DIM, (h + 1) * HEAD_DIM)
        kk = jnp.concatenate([kp_ref[:, hs], kc_ref[:, hs]], axis=0)
        vv = jnp.concatenate([vp_ref[:, hs], vc_ref[:, hs]], axis=0)
        vv1 = jnp.concatenate([vv, jnp.ones_like(vv)], axis=1)
        for g in range(SWA_GROUP):
            hq = h * SWA_GROUP + g
            slope = 2.0 ** (-8.0 * (hq + 1) / SWA_Q_HEADS)
            qs = slice(hq * HEAD_DIM, (hq + 1) * HEAD_DIM)
            s = lax.dot_general(q_ref[:, qs], kk, (((1,), (1,)), ((), ())),
                                preferred_element_type=F32) * scale
            logits = jnp.where(valid, s - slope * distf, NEG_INF)
            sink = sink_ref[hq]
            m = jnp.maximum(jnp.max(logits, -1, keepdims=True), sink)
            e = jnp.exp(logits - m).astype(BF16)
            ov = jnp.dot(e, vv1, preferred_element_type=F32)
            denom = ov[:, HEAD_DIM:] + jnp.exp(sink - m)
            o_ref[:, qs] = (ov[:, :HEAD_DIM] / denom).astype(o_ref.dtype)


def swa_attention(proj_sw, sinks):
    t = proj_sw.shape[0]
    nb = t // WINDOW
    kb, vb = SWA_Q_W // SWA_KV_W, SWA_Q_W // SWA_KV_W + 1
    cur = lambda blk: pl.BlockSpec((WINDOW, SWA_KV_W), lambda i: (i, blk))
    prev = lambda blk: pl.BlockSpec((WINDOW, SWA_KV_W), lambda i: (jnp.maximum(i - 1, 0), blk))
    return pl.pallas_call(
        _swa_body,
        grid=(nb,),
        in_specs=[pl.BlockSpec(memory_space=pltpu.SMEM),
                  pl.BlockSpec((WINDOW, SWA_Q_W), lambda i: (i, 0)),
                  cur(kb), prev(kb), cur(vb), prev(vb)],
        out_specs=pl.BlockSpec((WINDOW, SWA_Q_W), lambda i: (i, 0)),
        out_shape=jax.ShapeDtypeStruct((t, SWA_Q_W), BF16),
        compiler_params=_cparams(("parallel",), 32),
        name="swa",
    )(sinks.astype(F32), proj_sw, proj_sw, proj_sw, proj_sw, proj_sw)


def _top2_gates(x, w):
    logits = jnp.dot(x, w, precision=lax.Precision.HIGHEST, preferred_element_type=F32)
    lane = lax.broadcasted_iota(I32, logits.shape, 1)
    neg = jnp.float32(-jnp.inf)
    l1 = jnp.where(lane < N_EXPERTS, logits, neg)
    m1 = jnp.max(l1, -1, keepdims=True)
    i1 = jnp.min(jnp.where(l1 == m1, lane, LANES), -1, keepdims=True)
    l2 = jnp.where(lane == i1, neg, l1)
    m2 = jnp.max(l2, -1, keepdims=True)
    i2 = jnp.min(jnp.where(l2 == m2, lane, LANES), -1, keepdims=True)
    e = jnp.exp(m2 - m1)
    w1 = 1.0 / (1.0 + e)
    w2 = e * w1
    idx = jnp.where(lane == 0, i1, jnp.where(lane == 1, i2, 0))
    gates = jnp.where(lane == 0, w1, jnp.where(lane == 1, w2, 0.0))
    return idx, gates


def _add_ln_router_body(x_ref, y_ref, g_ref, b_ref, rw_ref, of_ref, idx_ref, gate_ref, *, alpha):
    r = alpha * x_ref[...] + y_ref[...].astype(F32)
    o = _layer_norm_rows(r, g_ref[...], b_ref[...])
    of_ref[...] = o
    idx_ref[...], gate_ref[...] = _top2_gates(o, rw_ref[...])


def add_layer_norm_router(x, y, g, b, alpha, router_w, *, bm=256):
    t, d = x.shape
    bm = _pick(t, bm)
    w = jnp.zeros((d, LANES), F32).at[:, :N_EXPERTS].set(router_w.astype(F32))
    row = lambda width: pl.BlockSpec((bm, width), lambda i: (i, 0))
    vec = pl.BlockSpec((1, d), lambda i: (0, 0))
    vmem_mib = (2 * bm * d * (4 + y.dtype.itemsize + 4) + 2 * d * LANES * 4 + 6 * bm * d * 4) // MIB + 8
    return pl.pallas_call(
        functools.partial(_add_ln_router_body, alpha=alpha),
        grid=(t // bm,),
        in_specs=[row(d), row(d), vec, vec, pl.BlockSpec((d, LANES), lambda i: (0, 0))],
        out_specs=[row(d), row(LANES), row(LANES)],
        out_shape=[jax.ShapeDtypeStruct((t, d), F32),
                   jax.ShapeDtypeStruct((t, LANES), I32), jax.ShapeDtypeStruct((t, LANES), F32)],
        compiler_params=_cparams(("parallel",), vmem_mib),
        name="add_ln_router",
    )(x, y, g.reshape(1, d), b.reshape(1, d), w)


def _route_plan(top_i, tm):
    t = top_i.shape[0]
    n_assign = 2 * t
    e_flat = top_i.reshape(-1)
    onehot = (e_flat[:, None] == jnp.arange(N_EXPERTS, dtype=I32)[None, :]).astype(I32)
    csum = jnp.cumsum(onehot, axis=0)
    rank = jnp.sum(csum * onehot, axis=1) - 1
    counts = csum[-1]
    padded = ((counts + tm - 1) // tm) * tm
    ends = jnp.cumsum(padded)
    offs = ends - padded
    pos = offs[e_flat] + rank
    n_tiles = n_assign // tm + N_EXPERTS
    row_src = jnp.zeros((n_tiles * tm,), I32).at[pos].set(jnp.arange(n_assign, dtype=I32) // 2)
    tile_start = jnp.arange(n_tiles, dtype=I32) * tm
    tile_e = jnp.sum((tile_start[:, None] >= ends[None, :]).astype(I32), axis=1)
    tile_valid = (tile_start < ends[-1]).astype(I32)
    last_e = jnp.max(jnp.where(padded > 0, jnp.arange(N_EXPERTS, dtype=I32), 0))
    tile_e = jnp.minimum(tile_e, last_e)
    return row_src, pos[0::2], pos[1::2], tile_e, tile_valid, n_tiles


_ISSUE_UNROLL = 8
_DMA_QUEUES = 2


def _gather_rows_body(src_ref, x_hbm, o_ref, buf, sem, *, rows):
    i = pl.program_id(0)
    slot = lax.rem(i, 2)

    def row_copy(step_slot, r, src_row):
        return pltpu.make_async_copy(x_hbm.at[pl.ds(src_row, 1), :],
                                     buf.at[step_slot, pl.ds(r, 1), :], sem.at[step_slot])

    def issue(step, step_slot):
        def body(p, carry):
            for q in range(_DMA_QUEUES):
                r = p * _DMA_QUEUES + q
                row_copy(step_slot, r, src_ref[step * rows + r]).start(priority=q)
            return carry
        lax.fori_loop(0, rows // _DMA_QUEUES, body, 0, unroll=_ISSUE_UNROLL // _DMA_QUEUES)

    @pl.when(i == 0)
    def _():
        issue(0, 0)

    @pl.when(i + 1 < pl.num_programs(0))
    def _():
        issue(i + 1, 1 - slot)

    def drain(r, carry):
        row_copy(slot, r, 0).wait()
        return carry

    lax.fori_loop(0, rows, drain, 0, unroll=_ISSUE_UNROLL)
    o_ref[...] = buf[slot].astype(o_ref.dtype)


def gather_rows(x, row_src, *, rows=512):
    d = x.shape[1]
    p = row_src.shape[0]
    rows = _pick(p, rows)
    return pl.pallas_call(
        functools.partial(_gather_rows_body, rows=rows),
        grid_spec=pltpu.PrefetchScalarGridSpec(
            num_scalar_prefetch=1,
            grid=(p // rows,),
            in_specs=[pl.BlockSpec(memory_space=pl.ANY)],
            out_specs=pl.BlockSpec((rows, d), lambda i, src: (i, 0)),
            scratch_shapes=[pltpu.VMEM((2, rows, d), F32), pltpu.SemaphoreType.DMA((2,))]),
        out_shape=jax.ShapeDtypeStruct((p, d), BF16),
        compiler_params=_cparams(("arbitrary",), (rows * d * (2 * 4 + 2 * 2 + 4)) // MIB + 6),
        name="moe_gather",
    )(row_src, x)


def _new_panel(te_ref, i):
    return (i == 0) | (te_ref[i] != te_ref[jnp.maximum(i - 1, 0)])


def _expert_up_body(te_ref, tv_ref, x_ref, w1_ref, w3_ref, h_ref, w1b_ref, w3b_ref):
    i = pl.program_id(1)

    @pl.when(_new_panel(te_ref, i))
    def _():
        w1b_ref[...] = w1_ref[...].astype(BF16)
        w3b_ref[...] = w3_ref[...].astype(BF16)

    @pl.when(tv_ref[i] == 1)
    def _():
        x = x_ref[...]
        a = jnp.dot(x, w1b_ref[...], preferred_element_type=F32)
        b = jnp.dot(x, w3b_ref[...], preferred_element_type=F32)
        h_ref[...] = (a * jax.nn.sigmoid(a) * b).astype(h_ref.dtype)

    @pl.when(tv_ref[i] == 0)
    def _():
        h_ref[...] = jnp.zeros_like(h_ref)


def expert_up(xs, w1, w3, layer, tile_e, tile_valid, tm, *, bn=512):
    p, d = xs.shape
    f = w1.shape[3]
    bn = _pick(f, bn)
    wspec = pl.BlockSpec((None, None, d, bn), lambda j, i, te, tv: (layer, te[i], 0, j))
    vmem = 2 * d * bn * (2 * 4 + 2) + 2 * (tm * d * 2 + tm * bn * 2) + 3 * tm * bn * 4
    return pl.pallas_call(
        _expert_up_body,
        grid_spec=pltpu.PrefetchScalarGridSpec(
            num_scalar_prefetch=2,
            grid=(f // bn, p // tm),
            in_specs=[pl.BlockSpec((tm, d), lambda j, i, te, tv: (i, 0)), wspec, wspec],
            out_specs=pl.BlockSpec((tm, bn), lambda j, i, te, tv: (i, j)),
            scratch_shapes=[pltpu.VMEM((d, bn), BF16), pltpu.VMEM((d, bn), BF16)]),
        out_shape=jax.ShapeDtypeStruct((p, f), BF16),
        compiler_params=_cparams(("arbitrary", "arbitrary"), vmem // MIB + 6),
        name="expert_up",
    )(tile_e, tile_valid, xs, w1, w3)


def _expert_down_body(te_ref, tv_ref, h_ref, w2_ref, y_ref, w2b_ref):
    i = pl.program_id(1)

    @pl.when(_new_panel(te_ref, i))
    def _():
        w2b_ref[...] = w2_ref[...].astype(BF16)

    @pl.when(tv_ref[i] == 1)
    def _():
        y_ref[...] = jnp.dot(h_ref[...], w2b_ref[...], preferred_element_type=F32).astype(y_ref.dtype)

    @pl.when(tv_ref[i] == 0)
    def _():
        y_ref[...] = jnp.zeros_like(y_ref)


def expert_down(h, w2, layer, tile_e, tile_valid, tm, *, bn=1024):
    p, f = h.shape
    d = w2.shape[3]
    bn = _pick(d, bn)
    vmem = f * bn * (2 * 4 + 2) + 2 * (tm * f * 2 + tm * bn * 4) + 2 * tm * bn * 4
    return pl.pallas_call(
        _expert_down_body,
        grid_spec=pltpu.PrefetchScalarGridSpec(
            num_scalar_prefetch=2,
            grid=(d // bn, p // tm),
            in_specs=[pl.BlockSpec((tm, f), lambda j, i, te, tv: (i, 0)),
                      pl.BlockSpec((None, None, f, bn), lambda j, i, te, tv: (layer, te[i], 0, j))],
            out_specs=pl.BlockSpec((tm, bn), lambda j, i, te, tv: (i, j)),
            scratch_shapes=[pltpu.VMEM((f, bn), BF16)]),
        out_shape=jax.ShapeDtypeStruct((p, d), F32),
        compiler_params=_cparams(("arbitrary", "arbitrary"), vmem // MIB + 6),
        name="expert_down",
    )(tile_e, tile_valid, h, w2)


def _combine_ln_body(p1_ref, p2_ref, y_hbm, x_ref, gate_ref, g_ref, b_ref, of_ref, ob_ref,
                     buf, sem, *, rows, alpha):
    i = pl.program_id(0)
    slot = lax.rem(i, 2)

    def row_copy(step_slot, which, r, src_row):
        return pltpu.make_async_copy(y_hbm.at[pl.ds(src_row, 1), :],
                                     buf.at[step_slot, which, pl.ds(r, 1), :], sem.at[step_slot])

    def issue(step, step_slot):
        def body(r, carry):
            row_copy(step_slot, 0, r, p1_ref[step * rows + r]).start()
            row_copy(step_slot, 1, r, p2_ref[step * rows + r]).start()
            return carry
        lax.fori_loop(0, rows, body, 0, unroll=_ISSUE_UNROLL)

    @pl.when(i == 0)
    def _():
        issue(0, 0)

    @pl.when(i + 1 < pl.num_programs(0))
    def _():
        issue(i + 1, 1 - slot)

    def drain(r, carry):
        row_copy(slot, 0, r, 0).wait()
        row_copy(slot, 1, r, 0).wait()
        return carry

    lax.fori_loop(0, rows, drain, 0, unroll=_ISSUE_UNROLL)
    gates = gate_ref[...]
    f = gates[:, 0:1] * buf[slot, 0] + gates[:, 1:2] * buf[slot, 1]
    o = _layer_norm_rows(alpha * x_ref[...] + f, g_ref[...], b_ref[...])
    of_ref[...] = o
    ob_ref[...] = o.astype(BF16)


def moe_combine_ln(y, pos1, pos2, gates, x, g, b, alpha, *, rows=128):
    t, d = x.shape
    rows = _pick(t, rows)
    row = lambda width: pl.BlockSpec((rows, width), lambda i, p1, p2: (i, 0))
    vec = pl.BlockSpec((1, d), lambda i, p1, p2: (0, 0))
    return pl.pallas_call(
        functools.partial(_combine_ln_body, rows=rows, alpha=alpha),
        grid_spec=pltpu.PrefetchScalarGridSpec(
            num_scalar_prefetch=2,
            grid=(t // rows,),
            in_specs=[pl.BlockSpec(memory_space=pl.ANY), row(d), row(LANES), vec, vec],
            out_specs=[row(d), row(d)],
            scratch_shapes=[pltpu.VMEM((2, 2, rows, d), F32), pltpu.SemaphoreType.DMA((2,))]),
        out_shape=[jax.ShapeDtypeStruct((t, d), F32), jax.ShapeDtypeStruct((t, d), BF16)],
        compiler_params=_cparams(("arbitrary",), (rows * d * (16 + 2 * (4 + 4 + 2) + 8)) // MIB + 6),
        name="moe_combine_ln",
    )(pos1, pos2, y, x, gates, g.reshape(1, d), b.reshape(1, d))


def moe_ffn_ln(x_f32, idx, gates, w1, w3, w2, layer, g, b, alpha, *, tm=512):
    row_src, pos1, pos2, tile_e, tile_valid, _ = _route_plan(idx[:, :2], tm)
    xs = gather_rows(x_f32, row_src, rows=tm)
    h = expert_up(xs, w1, w3, layer, tile_e, tile_valid, tm)
    y = expert_down(h, w2, layer, tile_e, tile_valid, tm)
    return moe_combine_ln(y, pos1, pos2, gates, x_f32, g, b, alpha)


def _gate_weights(wt_in, layer):
    d = wt_in.shape[2]
    pad = lambda rows: jnp.zeros((LANES, d), F32).at[:DN_HEADS].set(rows)
    return pad(wt_in[layer, _IN_Z:_IN_B, :]), pad(wt_in[layer, _IN_B:_IN_A, :])


def kernel(x, w_in, conv_w, a_log, dt_bias, dn_norm_w, sinks, w_o, ln1_g, ln1_b, ffn_w1, ffn_w3,
           ffn_w2, router_w, exp_w1, exp_w3, exp_w2, ln2_g, ln2_b):
    bsz, seq, d = x.shape
    depth = w_in.shape[0]
    alpha = (2 * depth) ** 0.25
    w_in, w_o, conv_w = w_in.astype(F32), w_o.astype(F32), conv_w.astype(F32)
    wt_in = jnp.swapaxes(w_in, 1, 2)
    wt_sw = wt_in[:, _IN_A:, :]
    outs = []
    for bi in range(bsz):
        xf = x[bi].astype(F32)
        xb = xf.astype(BF16)
        for i in range(depth):
            w_bt, w_at = _gate_weights(wt_in, i)
            qkv = in_proj(xb, wt_in, i, n_cols=_IN_QKV, conv_w=conv_w, name="in_proj_qkv")
            z = in_proj(xb, wt_in, i, row0=_IN_QKV, n_cols=DN_V_W, name="in_proj_z")
            proj_sw = in_proj(xb, wt_sw, i, name="in_proj_sw")
            beta, gam, gamt = dn_gates(xb, w_bt, w_at, a_log[i], dt_bias[i])
            out_a = delta_net(qkv, z, beta, gam, gamt, dn_norm_w[i])
            out_b = swa_attention(proj_sw, sinks[i])
            mix = matmul_ws([out_a, out_b], w_o, i, BF16, name="out_proj")
            j = i // 2
            if i % 2 == 0:
                xf, xb = add_layer_norm(xf, mix, ln1_g[i], ln1_b[i], alpha)
                h = swiglu_up(xb, ffn_w1.astype(F32), ffn_w3.astype(F32), j)
                f = matmul_ws([h], ffn_w2.astype(F32), j, BF16, bm=512, bn=512, name="ffn_down")
                xf, xb = add_layer_norm(xf, f, ln2_g[i], ln2_b[i], alpha)
            else:
                xf, idx, gates = add_layer_norm_router(xf, mix, ln1_g[i], ln1_b[i], alpha, router_w[j])
                xf, xb = moe_ffn_ln(xf, idx, gates, exp_w1.astype(F32), exp_w3.astype(F32),
                                    exp_w2.astype(F32), j, ln2_g[i], ln2_b[i], alpha)
        outs.append(xf)
    return jnp.stack(outs, axis=0).astype(x.dtype)
```

```python
import functools

import jax
import jax.numpy as jnp
from jax import lax
from jax.experimental import pallas as pl
from jax.experimental.pallas import tpu as pltpu

F32 = jnp.float32
BF16 = jnp.bfloat16
I32 = jnp.int32

HEAD_DIM = 128
DN_HEADS = 16
DN_QK_W = DN_HEADS * HEAD_DIM
DN_V_W = DN_HEADS * HEAD_DIM
CONV_K = 4
SWA_Q_HEADS = 16
SWA_KV_HEADS = 4
SWA_GROUP = SWA_Q_HEADS // SWA_KV_HEADS
SWA_Q_W = SWA_Q_HEADS * HEAD_DIM
SWA_KV_W = SWA_KV_HEADS * HEAD_DIM
WINDOW = 128
N_EXPERTS = 8
LN_EPS = 1e-5
NEG_INF = -1e30

_IN_QKV = 2 * DN_QK_W + DN_V_W
_IN_Z = _IN_QKV + DN_V_W
_IN_B = _IN_Z + DN_HEADS
_IN_A = _IN_B + DN_HEADS
Q_OFF, K_OFF, V_OFF, Z_OFF = 0, DN_QK_W, 2 * DN_QK_W, _IN_QKV
SW_W = SWA_Q_W + 2 * SWA_KV_W

LANES = 128
SUBLANES = 8
VMEM_BYTES_V7X = 64 * 1024 * 1024
MIB = 1024 * 1024

CHUNK = 128


def _cparams(semantics, vmem_mib):
    assert vmem_mib * MIB < VMEM_BYTES_V7X, vmem_mib
    return pltpu.CompilerParams(dimension_semantics=semantics, vmem_limit_bytes=vmem_mib * MIB)


def _pick(n, pref):
    t = min(n, pref)
    while n % t:
        t //= 2
    return t


def _resident(block_shape, index_map):
    return pl.BlockSpec(block_shape, index_map, pipeline_mode=pl.Buffered(1))


def _mm_ws_body(*refs, n_parts, part_k):
    a_refs = refs[:n_parts]
    w_ref, o_ref, wb_ref = refs[n_parts:]

    @pl.when(pl.program_id(1) == 0)
    def _():
        wb_ref[...] = w_ref[...].astype(BF16)

    acc = jnp.dot(a_refs[0][...], wb_ref[0:part_k, :], preferred_element_type=F32)
    for p in range(1, n_parts):
        acc = acc + jnp.dot(a_refs[p][...], wb_ref[p * part_k:(p + 1) * part_k, :],
                            preferred_element_type=F32)
    o_ref[...] = acc.astype(o_ref.dtype)


def matmul_ws(a_parts, w, layer, out_dtype, *, n_cols=None, bm=512, bn=1024, name="mm"):
    m, part_k = a_parts[0].shape
    n_parts = len(a_parts)
    k = part_k * n_parts
    assert w.shape[1] == k
    n_cols = w.shape[2] if n_cols is None else n_cols
    bm, bn = _pick(m, bm), _pick(n_cols, bn)
    osz = jnp.dtype(out_dtype).itemsize
    vmem = k * bn * (4 + 2) + 2 * (bm * k * 2 + bm * bn * osz) + 2 * bm * bn * 4
    return pl.pallas_call(
        functools.partial(_mm_ws_body, n_parts=n_parts, part_k=part_k),
        grid=(n_cols // bn, m // bm),
        in_specs=[pl.BlockSpec((bm, part_k), lambda j, i: (i, 0)) for _ in range(n_parts)]
        + [_resident((None, k, bn), lambda j, i: (layer, 0, j))],
        out_specs=pl.BlockSpec((bm, bn), lambda j, i: (i, j)),
        out_shape=jax.ShapeDtypeStruct((m, n_cols), out_dtype),
        scratch_shapes=[pltpu.VMEM((k, bn), BF16)],
        compiler_params=_cparams(("arbitrary", "arbitrary"), vmem // MIB + 6),
        name=name,
    )(*a_parts, w)


def _in_proj_body(a_ref, wt_ref, o_ref, wb_ref):
    @pl.when(pl.program_id(1) == 0)
    def _():
        wb_ref[...] = wt_ref[...].astype(BF16)

    o_ref[...] = lax.dot_general(a_ref[...], wb_ref[...], (((1,), (1,)), ((), ())),
                                 preferred_element_type=F32).astype(o_ref.dtype)


def _in_proj_qkv_body(a_ref, wt_ref, cw_ref, o_ref, wb_ref, ext_ref, *, bm, bn, sub, piece, q_panels,
                      qk_panels):
    j = pl.program_id(0)
    i = pl.program_id(1)
    halo = SUBLANES

    @pl.when(i == 0)
    def _():
        wb_ref[...] = wt_ref[...].astype(BF16)
        ext_ref[0:halo, :] = jnp.zeros((halo, bn), F32)

    w = cw_ref[...]
    q_scale = jnp.where(j < q_panels, HEAD_DIM ** -0.5, 1.0)
    for r0 in range(0, bm, sub):
        ext_ref[halo + r0:halo + r0 + sub, :] = lax.dot_general(
            a_ref[r0:r0 + sub, :], wb_ref[...], (((1,), (1,)), ((), ())), preferred_element_type=F32)
        for p0 in range(r0, r0 + sub, piece):
            for h in range(bn // HEAD_DIM):
                hs = slice(h * HEAD_DIM, (h + 1) * HEAD_DIM)
                y = ext_ref[halo + p0:halo + p0 + piece, hs] * w[CONV_K - 1:CONV_K, hs]
                for tap in range(CONV_K - 1):
                    y = y + ext_ref[pl.ds(halo + p0 - (CONV_K - 1) + tap, piece), hs] * w[tap:tap + 1, hs]
                y = y * jax.nn.sigmoid(y)
                inv = lax.rsqrt(jnp.sum(y * y, -1, keepdims=True) + 1e-6) * q_scale
                o_ref[p0:p0 + piece, hs] = (y * jnp.where(j < qk_panels, inv, 1.0)).astype(o_ref.dtype)
    ext_ref[0:halo, :] = ext_ref[bm:bm + halo, :]


def in_proj(a, wt, layer, *, row0=0, n_cols=None, conv_w=None, bm=1024, bn=512, name="in_proj"):
    m, k = a.shape
    n_cols = wt.shape[1] if n_cols is None else n_cols
    bm, bn = _pick(m, bm), _pick(n_cols, bn)
    assert row0 % bn == 0
    vmem = 2 * bn * k * 4 + bn * k * 2 + 2 * (bm * k * 2 + bm * bn * 2) + bm * bn * 4
    a_spec = pl.BlockSpec((bm, k), lambda j, i: (i, 0))
    w_spec = pl.BlockSpec((None, bn, k), lambda j, i: (layer, row0 // bn + j, 0))
    common = dict(
        grid=(n_cols // bn, m // bm),
        out_specs=pl.BlockSpec((bm, bn), lambda j, i: (i, j)),
        out_shape=jax.ShapeDtypeStruct((m, n_cols), BF16),
        name=name)
    if conv_w is None:
        return pl.pallas_call(
            _in_proj_body, in_specs=[a_spec, w_spec],
            scratch_shapes=[pltpu.VMEM((bn, k), BF16)],
            compiler_params=_cparams(("arbitrary", "arbitrary"), vmem // MIB + 6), **common)(a, wt)
    assert n_cols == _IN_QKV and DN_QK_W % bn == 0 and bn % HEAD_DIM == 0
    vmem += (SUBLANES + bm) * bn * 4 + 2 * bm * bn * 4
    return pl.pallas_call(
        functools.partial(_in_proj_qkv_body, bm=bm, bn=bn, sub=_pick(bm, 256), piece=_pick(bm, 64),
                          q_panels=DN_QK_W // bn, qk_panels=2 * DN_QK_W // bn),
        in_specs=[a_spec, w_spec, pl.BlockSpec((None, CONV_K, bn), lambda j, i: (layer, 0, j))],
        scratch_shapes=[pltpu.VMEM((bn, k), BF16), pltpu.VMEM((SUBLANES + bm, bn), F32)],
        compiler_params=_cparams(("arbitrary", "arbitrary"), vmem // MIB + 6), **common)(a, wt, conv_w)


def _layer_norm_rows(r, g, b):
    mu = jnp.mean(r, -1, keepdims=True)
    c = r - mu
    var = jnp.mean(c * c, -1, keepdims=True)
    return c * lax.rsqrt(var + LN_EPS) * g + b


def _add_ln_body(x_ref, y_ref, g_ref, b_ref, of_ref, ob_ref, *, alpha):
    r = alpha * x_ref[...] + y_ref[...].astype(F32)
    o = _layer_norm_rows(r, g_ref[...], b_ref[...])
    of_ref[...] = o
    ob_ref[...] = o.astype(BF16)


def add_layer_norm(x, y, g, b, alpha, *, bm=256):
    t, d = x.shape
    bm = _pick(t, bm)
    row = pl.BlockSpec((bm, d), lambda i: (i, 0))
    vec = pl.BlockSpec((1, d), lambda i: (0, 0))
    vmem_mib = (2 * bm * d * (4 + y.dtype.itemsize + 4 + 2)) // MIB + 8
    return pl.pallas_call(
        functools.partial(_add_ln_body, alpha=alpha),
        grid=(t // bm,),
        in_specs=[row, row, vec, vec],
        out_specs=[row, row],
        out_shape=[jax.ShapeDtypeStruct((t, d), F32), jax.ShapeDtypeStruct((t, d), BF16)],
        compiler_params=_cparams(("parallel",), vmem_mib),
        name="add_ln",
    )(x, y, g.reshape(1, d), b.reshape(1, d))


def _swiglu_up_body(x_ref, w1_ref, w3_ref, h_ref, w1b_ref, w3b_ref):
    @pl.when(pl.program_id(1) == 0)
    def _():
        w1b_ref[...] = w1_ref[...].astype(BF16)
        w3b_ref[...] = w3_ref[...].astype(BF16)

    x = x_ref[...]
    a = jnp.dot(x, w1b_ref[...], preferred_element_type=F32)
    b = jnp.dot(x, w3b_ref[...], preferred_element_type=F32)
    h_ref[...] = (a * jax.nn.sigmoid(a) * b).astype(h_ref.dtype)


def swiglu_up(x, w1, w3, layer, *, bm=1024, bn=512):
    t, d = x.shape
    f = w1.shape[2]
    bm, bn = _pick(t, bm), _pick(f, bn)
    wspec = _resident((None, d, bn), lambda j, i: (layer, 0, j))
    vmem = 2 * d * bn * (4 + 2) + 2 * (bm * d * 2 + bm * bn * 2) + 4 * bm * bn * 4
    return pl.pallas_call(
        _swiglu_up_body,
        grid=(f // bn, t // bm),
        in_specs=[pl.BlockSpec((bm, d), lambda j, i: (i, 0)), wspec, wspec],
        out_specs=pl.BlockSpec((bm, bn), lambda j, i: (i, j)),
        out_shape=jax.ShapeDtypeStruct((t, f), BF16),
        scratch_shapes=[pltpu.VMEM((d, bn), BF16), pltpu.VMEM((d, bn), BF16)],
        compiler_params=_cparams(("arbitrary", "arbitrary"), vmem // MIB + 6),
        name="swiglu_up",
    )(x, w1, w3)


def _softplus(x):
    return jnp.maximum(x, 0.0) + jnp.log(1.0 + jnp.exp(-jnp.abs(x)))


def _dn_gate_body(x_ref, wbt_ref, wat_ref, alog_r_ref, dtb_r_ref, alog_c_ref, dtb_c_ref,
                  beta_ref, gam_ref, gamt_ref, *, tb):
    x = x_ref[...]
    nt = (((1,), (1,)), ((), ()))
    wbt = wbt_ref[...].astype(BF16)
    wat = wat_ref[...].astype(BF16)
    beta_ref[...] = jax.nn.sigmoid(lax.dot_general(x, wbt, nt, preferred_element_type=F32))
    pa = lax.dot_general(x, wat, nt, preferred_element_type=F32)
    g = -jnp.exp(alog_r_ref[...]) * _softplus(pa + dtb_r_ref[...])
    pt = lax.dot_general(wat[:DN_HEADS], x, nt, preferred_element_type=F32)
    gt = -jnp.exp(alog_c_ref[...]) * _softplus(pt + dtb_c_ref[...])
    r = lax.broadcasted_iota(I32, (tb, tb), 0)
    c = lax.broadcasted_iota(I32, (tb, tb), 1)
    same = (r // CHUNK) == (c // CHUNK)
    lower = jnp.where(same & (r >= c), 1.0, 0.0).astype(F32)
    upper = jnp.where(same & (r <= c), 1.0, 0.0).astype(F32)
    gam_ref[...] = jnp.dot(lower, g, precision=lax.Precision.HIGHEST, preferred_element_type=F32)
    gamt_ref[...] = jnp.dot(gt, upper, precision=lax.Precision.HIGHEST, preferred_element_type=F32)


def dn_gates(x, wbt, wat, a_log, dt_bias, *, tb=512):
    t, d = x.shape
    tb = _pick(t, tb)
    pad = lambda v: jnp.zeros((1, LANES), F32).at[0, :DN_HEADS].set(v.astype(F32))
    col = lambda v: v.astype(F32).reshape(DN_HEADS, 1)
    full = lambda shape: pl.BlockSpec(shape, lambda i: (0, 0))
    return pl.pallas_call(
        functools.partial(_dn_gate_body, tb=tb),
        grid=(t // tb,),
        in_specs=[pl.BlockSpec((tb, d), lambda i: (i, 0)),
                  full((LANES, d)), full((LANES, d)),
                  full((1, LANES)), full((1, LANES)), full((DN_HEADS, 1)), full((DN_HEADS, 1))],
        out_specs=[pl.BlockSpec((tb, LANES), lambda i: (i, 0)),
                   pl.BlockSpec((tb, LANES), lambda i: (i, 0)),
                   pl.BlockSpec((DN_HEADS, tb), lambda i: (0, i))],
        out_shape=[jax.ShapeDtypeStruct((t, LANES), F32),
                   jax.ShapeDtypeStruct((t, LANES), F32),
                   jax.ShapeDtypeStruct((DN_HEADS, t), F32)],
        compiler_params=_cparams(("parallel",), 32),
        name="dn_gates",
    )(x, wbt, wat, pad(a_log), pad(dt_bias), col(a_log), col(dt_bias))


def _mm16(a, b):
    return jnp.dot(a.astype(BF16), b.astype(BF16), preferred_element_type=F32)


def _mm16_each(xs, ys):
    return [_mm16(x, y) for x, y in zip(xs, ys)]


def _unit_lower_inverse_each(a_list, eye, diag_blocks):
    d = [jnp.where(diag_blocks, a, 0.0) for a in a_list]
    n = [a - di for a, di in zip(a_list, d)]
    d2 = _mm16_each(d, d)
    x = _mm16_each([eye - t for t in d], [eye + t for t in d2])
    d4 = _mm16_each(d2, d2)
    x = _mm16_each(x, [eye + t for t in d4])
    d8 = _mm16_each(d4, d4)
    x = _mm16_each(x, [eye + t for t in d8])
    m = _mm16_each(x, n)
    m2 = _mm16_each(m, m)
    y = _mm16_each([eye - t for t in m], [eye + t for t in m2])
    m4 = _mm16_each(m2, m2)
    y = _mm16_each(y, [eye + t for t in m4])
    return _mm16_each(y, x)


def _delta_body(q_ref, k_ref, v_ref, z_ref, beta_ref, gam_ref, gamt_ref, nw_ref, o_ref, s_ref, *,
                hb, tb):
    hg = pl.program_id(0)
    t = pl.program_id(1)

    @pl.when(t == 0)
    def _():
        s_ref[...] = jnp.zeros_like(s_ref)

    row = lax.broadcasted_iota(I32, (CHUNK, CHUNK), 0)
    col = lax.broadcasted_iota(I32, (CHUNK, CHUNK), 1)
    causal = row >= col
    strict = row > col
    diag_blocks = (row // 16) == (col // 16)
    eye = jnp.where(row == col, 1.0, 0.0).astype(F32)
    lane = lax.broadcasted_iota(I32, (tb, LANES), 1)
    nw = nw_ref[...]
    heads = range(hb)
    sq = (CHUNK, CHUNK)

    bcol, gcol, grow = [], [], []
    for hl in heads:
        head = hg * hb + hl
        sel = lane == head
        bcol.append(jnp.sum(jnp.where(sel, beta_ref[...], 0.0), axis=-1, keepdims=True))
        gcol.append(jnp.sum(jnp.where(sel, gam_ref[...], 0.0), axis=-1, keepdims=True))
        grow.append(gamt_ref[pl.ds(head, 1), :])

    for c in range(tb // CHUNK):
        rs = slice(c * CHUNK, (c + 1) * CHUNK)
        cs = [slice(hl * HEAD_DIM, (hl + 1) * HEAD_DIM) for hl in heads]
        q = [q_ref[rs, cs[h]].astype(F32) for h in heads]
        k = [k_ref[rs, cs[h]].astype(F32) for h in heads]
        v = [v_ref[rs, cs[h]].astype(F32) for h in heads]
        bc = [jnp.broadcast_to(bcol[h][rs, :], sq) for h in heads]
        gc = [jnp.broadcast_to(gcol[h][rs, :], sq) for h in heads]
        gr = [jnp.broadcast_to(grow[h][:, rs], sq) for h in heads]
        g_last = [jnp.broadcast_to(grow[h][:, (c + 1) * CHUNK - 1:(c + 1) * CHUNK], sq) for h in heads]
        decay = [jnp.exp(jnp.where(causal, gc[h] - gr[h], NEG_INF)) for h in heads]
        egc = [jnp.exp(gc[h]) for h in heads]
        kt = [k[h].T for h in heads]
        gram = _mm16_each([jnp.concatenate([q[h], k[h]], axis=0) for h in heads], kt)
        a_qk = [gram[h][:CHUNK] * decay[h] for h in heads]
        a_kk = [jnp.where(strict, bc[h] * gram[h][CHUNK:] * decay[h], 0.0) for h in heads]
        tinv = _unit_lower_inverse_each(a_kk, eye, diag_blocks)
        sol = _mm16_each(tinv, [jnp.concatenate([v[h] * bc[h], k[h] * (bc[h] * egc[h])], axis=1)
                                for h in heads])
        lhs1 = [jnp.concatenate([sol[h][:, HEAD_DIM:], q[h] * egc[h]], axis=0) for h in heads]
        kdt = [kt[h] * jnp.exp(g_last[h] - gr[h]) for h in heads]
        s = [s_ref[h] for h in heads]
        ws_qs = _mm16_each(lhs1, s)
        v_new = [sol[h][:, :HEAD_DIM] - ws_qs[h][:CHUNK] for h in heads]
        av_ds = _mm16_each([jnp.concatenate([a_qk[h], kdt[h]], axis=0) for h in heads], v_new)
        for h in heads:
            s_ref[h] = s[h] * jnp.exp(g_last[h]) + av_ds[h][CHUNK:]
            o = ws_qs[h][CHUNK:] + av_ds[h][:CHUNK]
            o = o * lax.rsqrt(jnp.mean(o * o, -1, keepdims=True) + 1e-6) * nw
            zz = z_ref[rs, cs[h]].astype(F32)
            o_ref[rs, cs[h]] = (o * (zz * jax.nn.sigmoid(zz))).astype(o_ref.dtype)


def delta_net(qkv, z, beta, gam, gamt, norm_w, *, hb=8, tb=512):
    t = qkv.shape[0]
    tb = _pick(t, tb)
    w = hb * HEAD_DIM
    pspec = lambda off: pl.BlockSpec((tb, w), lambda h, i: (i, off // w + h))
    return pl.pallas_call(
        functools.partial(_delta_body, hb=hb, tb=tb),
        grid=(DN_HEADS // hb, t // tb),
        in_specs=[pspec(Q_OFF), pspec(K_OFF), pspec(V_OFF), pspec(0),
                  pl.BlockSpec((tb, LANES), lambda h, i: (i, 0)),
                  pl.BlockSpec((tb, LANES), lambda h, i: (i, 0)),
                  pl.BlockSpec((DN_HEADS, tb), lambda h, i: (0, i)),
                  pl.BlockSpec((1, HEAD_DIM), lambda h, i: (0, 0))],
        out_specs=pl.BlockSpec((tb, w), lambda h, i: (i, h)),
        out_shape=jax.ShapeDtypeStruct((t, DN_V_W), BF16),
        scratch_shapes=[pltpu.VMEM((hb, HEAD_DIM, HEAD_DIM), F32)],
        compiler_params=_cparams(("arbitrary", "arbitrary"), 48),
        name="delta_net",
    )(qkv, qkv, qkv, z, beta, gam, gamt, norm_w.astype(F32).reshape(1, HEAD_DIM))


_SWA_QB = 2


def _swa_body(sink_ref, q_ref, kc_ref, kp_ref, vc_ref, vp_ref, o_ref):
    i = pl.program_id(0)
    qi = lax.broadcasted_iota(I32, (WINDOW, 2 * WINDOW), 0)
    kj = lax.broadcasted_iota(I32, (WINDOW, 2 * WINDOW), 1)
    dist = qi - kj + WINDOW
    in_window = (dist >= 0) & (dist < WINDOW)
    first_key = jnp.where(i > 0, 0, WINDOW)
    valid_first = in_window & (kj >= first_key)
    distf = dist.astype(F32)
    scale = HEAD_DIM ** -0.5
    for h in range(SWA_KV_HEADS):
        hs = slice(h * HEAD_DIM, (h + 1) * HEAD_DIM)
        for b in range(_SWA_QB):
            rows = slice(b * WINDOW, (b + 1) * WINDOW)
            if b == 0:
                kk = jnp.concatenate([kp_ref[:, hs], kc_ref[rows, hs]], axis=0)
                vv = jnp.concatenate([vp_ref[:, hs], vc_ref[rows, hs]], axis=0)
                valid = valid_first
            else:
                kk = kc_ref[(b - 1) * WINDOW:(b + 1) * WINDOW, hs]
                vv = vc_ref[(b - 1) * WINDOW:(b + 1) * WINDOW, hs]
                valid = in_window
            vv1 = jnp.concatenate([vv, jnp.ones_like(vv)], axis=1)
            for g in range(SWA_GROUP):
                hq = h * SWA_GROUP + g
                slope = 2.0 ** (-8.0 * (hq + 1) / SWA_Q_HEADS)
                qs = slice(hq * HEAD_DIM, (hq + 1) * HEAD_DIM)
                s = lax.dot_general(q_ref[rows, qs], kk, (((1,), (1,)), ((), ())),
                                    preferred_element_type=F32) * scale
                logits = jnp.where(valid, s - slope * distf, NEG_INF)
                sink = sink_ref[hq]
                m = jnp.maximum(jnp.max(logits, -1, keepdims=True), sink)
                e = jnp.exp(logits - m).astype(BF16)
                ov = jnp.dot(e, vv1, preferred_element_type=F32)
                denom = ov[:, HEAD_DIM:] + jnp.exp(sink - m)
                o_ref[rows, qs] = (ov[:, :HEAD_DIM] / denom).astype(o_ref.dtype)


def swa_attention(proj_sw, sinks):
    t = proj_sw.shape[0]
    qrows = _SWA_QB * WINDOW
    assert t % qrows == 0
    kb, vb = SWA_Q_W // SWA_KV_W, SWA_Q_W // SWA_KV_W + 1
    cur = lambda blk: pl.BlockSpec((qrows, SWA_KV_W), lambda i: (i, blk))
    prev = lambda blk: pl.BlockSpec((WINDOW, SWA_KV_W), lambda i: (jnp.maximum(_SWA_QB * i - 1, 0), blk))
    return pl.pallas_call(
        _swa_body,
        grid=(t // qrows,),
        in_specs=[pl.BlockSpec(memory_space=pltpu.SMEM),
                  pl.BlockSpec((qrows, SWA_Q_W), lambda i: (i, 0)),
                  cur(kb), prev(kb), cur(vb), prev(vb)],
        out_specs=pl.BlockSpec((qrows, SWA_Q_W), lambda i: (i, 0)),
        out_shape=jax.ShapeDtypeStruct((t, SWA_Q_W), BF16),
        compiler_params=_cparams(("parallel",), 32),
        name="swa",
    )(sinks.astype(F32), proj_sw, proj_sw, proj_sw, proj_sw, proj_sw)


def _top2_gates(x, w):
    logits = jnp.dot(x, w, precision=lax.Precision.HIGHEST, preferred_element_type=F32)
    lane = lax.broadcasted_iota(I32, logits.shape, 1)
    neg = jnp.float32(-jnp.inf)
    l1 = jnp.where(lane < N_EXPERTS, logits, neg)
    m1 = jnp.max(l1, -1, keepdims=True)
    i1 = jnp.min(jnp.where(l1 == m1, lane, LANES), -1, keepdims=True)
    l2 = jnp.where(lane == i1, neg, l1)
    m2 = jnp.max(l2, -1, keepdims=True)
    i2 = jnp.min(jnp.where(l2 == m2, lane, LANES), -1, keepdims=True)
    e = jnp.exp(m2 - m1)
    w1 = 1.0 / (1.0 + e)
    w2 = e * w1
    idx = jnp.where(lane == 0, i1, jnp.where(lane == 1, i2, 0))
    gates = jnp.where(lane == 0, w1, jnp.where(lane == 1, w2, 0.0))
    return idx, gates


def _add_ln_router_body(x_ref, y_ref, g_ref, b_ref, rw_ref, of_ref, idx_ref, gate_ref, *, alpha):
    r = alpha * x_ref[...] + y_ref[...].astype(F32)
    o = _layer_norm_rows(r, g_ref[...], b_ref[...])
    of_ref[...] = o
    idx_ref[...], gate_ref[...] = _top2_gates(o, rw_ref[...])


def add_layer_norm_router(x, y, g, b, alpha, router_w, *, bm=256):
    t, d = x.shape
    bm = _pick(t, bm)
    w = jnp.zeros((d, LANES), F32).at[:, :N_EXPERTS].set(router_w.astype(F32))
    row = lambda width: pl.BlockSpec((bm, width), lambda i: (i, 0))
    vec = pl.BlockSpec((1, d), lambda i: (0, 0))
    vmem_mib = (2 * bm * d * (4 + y.dtype.itemsize + 4) + 2 * d * LANES * 4 + 6 * bm * d * 4) // MIB + 8
    return pl.pallas_call(
        functools.partial(_add_ln_router_body, alpha=alpha),
        grid=(t // bm,),
        in_specs=[row(d), row(d), vec, vec, pl.BlockSpec((d, LANES), lambda i: (0, 0))],
        out_specs=[row(d), row(LANES), row(LANES)],
        out_shape=[jax.ShapeDtypeStruct((t, d), F32),
                   jax.ShapeDtypeStruct((t, LANES), I32), jax.ShapeDtypeStruct((t, LANES), F32)],
        compiler_params=_cparams(("parallel",), vmem_mib),
        name="add_ln_router",
    )(x, y, g.reshape(1, d), b.reshape(1, d), w)


def _route_plan(top_i, tm):
    t = top_i.shape[0]
    n_assign = 2 * t
    e_flat = top_i.reshape(-1)
    onehot = (e_flat[:, None] == jnp.arange(N_EXPERTS, dtype=I32)[None, :]).astype(I32)
    csum = jnp.cumsum(onehot, axis=0)
    rank = jnp.sum(csum * onehot, axis=1) - 1
    counts = csum[-1]
    padded = ((counts + tm - 1) // tm) * tm
    ends = jnp.cumsum(padded)
    offs = ends - padded
    pos = offs[e_flat] + rank
    n_tiles = n_assign // tm + N_EXPERTS
    row_src = jnp.zeros((n_tiles * tm,), I32).at[pos].set(jnp.arange(n_assign, dtype=I32) // 2)
    tile_start = jnp.arange(n_tiles, dtype=I32) * tm
    tile_e = jnp.sum((tile_start[:, None] >= ends[None, :]).astype(I32), axis=1)
    tile_valid = (tile_start < ends[-1]).astype(I32)
    last_e = jnp.max(jnp.where(padded > 0, jnp.arange(N_EXPERTS, dtype=I32), 0))
    tile_e = jnp.minimum(tile_e, last_e)
    return row_src, pos[0::2], pos[1::2], tile_e, tile_valid, n_tiles


_ISSUE_UNROLL = 8
_DMA_QUEUES = 2


def _gather_rows_body(src_ref, x_hbm, o_ref, buf, sem, *, rows):
    i = pl.program_id(0)
    slot = lax.rem(i, 2)

    def row_copy(step_slot, r, src_row):
        return pltpu.make_async_copy(x_hbm.at[pl.ds(src_row, 1), :],
                                     buf.at[step_slot, pl.ds(r, 1), :], sem.at[step_slot])

    def issue(step, step_slot):
        def body(p, carry):
            for q in range(_DMA_QUEUES):
                r = p * _DMA_QUEUES + q
                row_copy(step_slot, r, src_ref[step * rows + r]).start(priority=q)
            return carry
        lax.fori_loop(0, rows // _DMA_QUEUES, body, 0, unroll=_ISSUE_UNROLL // _DMA_QUEUES)

    @pl.when(i == 0)
    def _():
        issue(0, 0)

    @pl.when(i + 1 < pl.num_programs(0))
    def _():
        issue(i + 1, 1 - slot)

    def drain(r, carry):
        row_copy(slot, r, 0).wait()
        return carry

    lax.fori_loop(0, rows, drain, 0, unroll=_ISSUE_UNROLL)
    o_ref[...] = buf[slot].astype(o_ref.dtype)


def gather_rows(x, row_src, *, rows=512):
    d = x.shape[1]
    p = row_src.shape[0]
    rows = _pick(p, rows)
    return pl.pallas_call(
        functools.partial(_gather_rows_body, rows=rows),
        grid_spec=pltpu.PrefetchScalarGridSpec(
            num_scalar_prefetch=1,
            grid=(p // rows,),
            in_specs=[pl.BlockSpec(memory_space=pl.ANY)],
            out_specs=pl.BlockSpec((rows, d), lambda i, src: (i, 0)),
            scratch_shapes=[pltpu.VMEM((2, rows, d), F32), pltpu.SemaphoreType.DMA((2,))]),
        out_shape=jax.ShapeDtypeStruct((p, d), BF16),
        compiler_params=_cparams(("arbitrary",), (rows * d * (2 * 4 + 2 * 2 + 4)) // MIB + 6),
        name="moe_gather",
    )(row_src, x)


def _new_panel(te_ref, i):
    return (i == 0) | (te_ref[i] != te_ref[jnp.maximum(i - 1, 0)])


def _expert_up_body(te_ref, tv_ref, x_ref, w1_ref, w3_ref, h_ref, w1b_ref, w3b_ref):
    i = pl.program_id(1)

    @pl.when(_new_panel(te_ref, i))
    def _():
        w1b_ref[...] = w1_ref[...].astype(BF16)
        w3b_ref[...] = w3_ref[...].astype(BF16)

    @pl.when(tv_ref[i] == 1)
    def _():
        x = x_ref[...]
        a = jnp.dot(x, w1b_ref[...], preferred_element_type=F32)
        b = jnp.dot(x, w3b_ref[...], preferred_element_type=F32)
        h_ref[...] = (a * jax.nn.sigmoid(a) * b).astype(h_ref.dtype)

    @pl.when(tv_ref[i] == 0)
    def _():
        h_ref[...] = jnp.zeros_like(h_ref)


def expert_up(xs, w1, w3, layer, tile_e, tile_valid, tm, *, bn=512):
    p, d = xs.shape
    f = w1.shape[3]
    bn = _pick(f, bn)
    wspec = pl.BlockSpec((None, None, d, bn), lambda j, i, te, tv: (layer, te[i], 0, j))
    vmem = 2 * d * bn * (2 * 4 + 2) + 2 * (tm * d * 2 + tm * bn * 2) + 3 * tm * bn * 4
    return pl.pallas_call(
        _expert_up_body,
        grid_spec=pltpu.PrefetchScalarGridSpec(
            num_scalar_prefetch=2,
            grid=(f // bn, p // tm),
            in_specs=[pl.BlockSpec((tm, d), lambda j, i, te, tv: (i, 0)), wspec, wspec],
            out_specs=pl.BlockSpec((tm, bn), lambda j, i, te, tv: (i, j)),
            scratch_shapes=[pltpu.VMEM((d, bn), BF16), pltpu.VMEM((d, bn), BF16)]),
        out_shape=jax.ShapeDtypeStruct((p, f), BF16),
        compiler_params=_cparams(("arbitrary", "arbitrary"), vmem // MIB + 6),
        name="expert_up",
    )(tile_e, tile_valid, xs, w1, w3)


def _expert_down_body(te_ref, tv_ref, h_ref, w2_ref, y_ref, w2b_ref):
    i = pl.program_id(1)

    @pl.when(_new_panel(te_ref, i))
    def _():
        w2b_ref[...] = w2_ref[...].astype(BF16)

    @pl.when(tv_ref[i] == 1)
    def _():
        y_ref[...] = jnp.dot(h_ref[...], w2b_ref[...], preferred_element_type=F32).astype(y_ref.dtype)

    @pl.when(tv_ref[i] == 0)
    def _():
        y_ref[...] = jnp.zeros_like(y_ref)


def expert_down(h, w2, layer, tile_e, tile_valid, tm, *, bn=1024):
    p, f = h.shape
    d = w2.shape[3]
    bn = _pick(d, bn)
    vmem = f * bn * (2 * 4 + 2) + 2 * (tm * f * 2 + tm * bn * 4) + 2 * tm * bn * 4
    return pl.pallas_call(
        _expert_down_body,
        grid_spec=pltpu.PrefetchScalarGridSpec(
            num_scalar_prefetch=2,
            grid=(d // bn, p // tm),
            in_specs=[pl.BlockSpec((tm, f), lambda j, i, te, tv: (i, 0)),
                      pl.BlockSpec((None, None, f, bn), lambda j, i, te, tv: (layer, te[i], 0, j))],
            out_specs=pl.BlockSpec((tm, bn), lambda j, i, te, tv: (i, j)),
            scratch_shapes=[pltpu.VMEM((f, bn), BF16)]),
        out_shape=jax.ShapeDtypeStruct((p, d), F32),
        compiler_params=_cparams(("arbitrary", "arbitrary"), vmem // MIB + 6),
        name="expert_down",
    )(tile_e, tile_valid, h, w2)


def _combine_ln_body(p1_ref, p2_ref, y_hbm, x_ref, gate_ref, g_ref, b_ref, of_ref, ob_ref,
                     buf, sem, *, rows, alpha):
    i = pl.program_id(0)
    slot = lax.rem(i, 2)

    def row_copy(step_slot, which, r, src_row):
        return pltpu.make_async_copy(y_hbm.at[pl.ds(src_row, 1), :],
                                     buf.at[step_slot, which, pl.ds(r, 1), :], sem.at[step_slot])

    def issue(step, step_slot):
        def body(r, carry):
            row_copy(step_slot, 0, r, p1_ref[step * rows + r]).start()
            row_copy(step_slot, 1, r, p2_ref[step * rows + r]).start()
            return carry
        lax.fori_loop(0, rows, body, 0, unroll=_ISSUE_UNROLL)

    @pl.when(i == 0)
    def _():
        issue(0, 0)

    @pl.when(i + 1 < pl.num_programs(0))
    def _():
        issue(i + 1, 1 - slot)

    def drain(r, carry):
        row_copy(slot, 0, r, 0).wait()
        row_copy(slot, 1, r, 0).wait()
        return carry

    lax.fori_loop(0, rows, drain, 0, unroll=_ISSUE_UNROLL)
    gates = gate_ref[...]
    f = gates[:, 0:1] * buf[slot, 0] + gates[:, 1:2] * buf[slot, 1]
    o = _layer_norm_rows(alpha * x_ref[...] + f, g_ref[...], b_ref[...])
    of_ref[...] = o
    ob_ref[...] = o.astype(BF16)


def moe_combine_ln(y, pos1, pos2, gates, x, g, b, alpha, *, rows=128):
    t, d = x.shape
    rows = _pick(t, rows)
    row = lambda width: pl.BlockSpec((rows, width), lambda i, p1, p2: (i, 0))
    vec = pl.BlockSpec((1, d), lambda i, p1, p2: (0, 0))
    return pl.pallas_call(
        functools.partial(_combine_ln_body, rows=rows, alpha=alpha),
        grid_spec=pltpu.PrefetchScalarGridSpec(
            num_scalar_prefetch=2,
            grid=(t // rows,),
            in_specs=[pl.BlockSpec(memory_space=pl.ANY), row(d), row(LANES), vec, vec],
            out_specs=[row(d), row(d)],
            scratch_shapes=[pltpu.VMEM((2, 2, rows, d), F32), pltpu.SemaphoreType.DMA((2,))]),
        out_shape=[jax.ShapeDtypeStruct((t, d), F32), jax.ShapeDtypeStruct((t, d), BF16)],
        compiler_params=_cparams(("arbitrary",), (rows * d * (16 + 2 * (4 + 4 + 2) + 8)) // MIB + 6),
        name="moe_combine_ln",
    )(pos1, pos2, y, x, gates, g.reshape(1, d), b.reshape(1, d))


def moe_ffn_ln(x_f32, idx, gates, w1, w3, w2, layer, g, b, alpha, *, tm=512):
    row_src, pos1, pos2, tile_e, tile_valid, _ = _route_plan(idx[:, :2], tm)
    xs = gather_rows(x_f32, row_src, rows=tm)
    h = expert_up(xs, w1, w3, layer, tile_e, tile_valid, tm)
    y = expert_down(h, w2, layer, tile_e, tile_valid, tm)
    return moe_combine_ln(y, pos1, pos2, gates, x_f32, g, b, alpha)


def _gate_weights(wt_in, layer):
    d = wt_in.shape[2]
    pad = lambda rows: jnp.zeros((LANES, d), F32).at[:DN_HEADS].set(rows)
    return pad(wt_in[layer, _IN_Z:_IN_B, :]), pad(wt_in[layer, _IN_B:_IN_A, :])


def kernel(x, w_in, conv_w, a_log, dt_bias, dn_norm_w, sinks, w_o, ln1_g, ln1_b, ffn_w1, ffn_w3,
           ffn_w2, router_w, exp_w1, exp_w3, exp_w2, ln2_g, ln2_b):
    bsz, seq, d = x.shape
    depth = w_in.shape[0]
    alpha = (2 * depth) ** 0.25
    w_in, w_o, conv_w = w_in.astype(F32), w_o.astype(F32), conv_w.astype(F32)
    wt_in = jnp.swapaxes(w_in, 1, 2)
    wt_sw = wt_in[:, _IN_A:, :]
    outs = []
    for bi in range(bsz):
        xf = x[bi].astype(F32)
        xb = xf.astype(BF16)
        for i in range(depth):
            w_bt, w_at = _gate_weights(wt_in, i)
            qkv = in_proj(xb, wt_in, i, n_cols=_IN_QKV, conv_w=conv_w, name="in_proj_qkv")
            z = in_proj(xb, wt_in, i, row0=_IN_QKV, n_cols=DN_V_W, name="in_proj_z")
            proj_sw = in_proj(xb, wt_sw, i, name="in_proj_sw")
            beta, gam, gamt = dn_gates(xb, w_bt, w_at, a_log[i], dt_bias[i])
            out_a = delta_net(qkv, z, beta, gam, gamt, dn_norm_w[i])
            out_b = swa_attention(proj_sw, sinks[i])
            mix = matmul_ws([out_a, out_b], w_o, i, BF16, name="out_proj")
            j = i // 2
            if i % 2 == 0:
                xf, xb = add_layer_norm(xf, mix, ln1_g[i], ln1_b[i], alpha)
                h = swiglu_up(xb, ffn_w1.astype(F32), ffn_w3.astype(F32), j)
                f = matmul_ws([h], ffn_w2.astype(F32), j, BF16, bm=512, bn=512, name="ffn_down")
                xf, xb = add_layer_norm(xf, f, ln2_g[i], ln2_b[i], alpha)
            else:
                xf, idx, gates = add_layer_norm_router(xf, mix, ln1_g[i], ln1_b[i], alpha, router_w[j])
                xf, xb = moe_ffn_ln(xf, idx, gates, exp_w1.astype(F32), exp_w3.astype(F32),
                                    exp_w2.astype(F32), j, ln2_g[i], ln2_b[i], alpha)
        outs.append(xf)
    return jnp.stack(outs, axis=0).astype(x.dtype)
```

```python
import functools

import jax
import jax.numpy as jnp
from jax import lax
from jax.experimental import pallas as pl
from jax.experimental.pallas import tpu as pltpu

F32 = jnp.float32
BF16 = jnp.bfloat16
I32 = jnp.int32

HEAD_DIM = 128
DN_HEADS = 16
DN_QK_W = DN_HEADS * HEAD_DIM
DN_V_W = DN_HEADS * HEAD_DIM
CONV_K = 4
SWA_Q_HEADS = 16
SWA_KV_HEADS = 4
SWA_GROUP = SWA_Q_HEADS // SWA_KV_HEADS
SWA_Q_W = SWA_Q_HEADS * HEAD_DIM
SWA_KV_W = SWA_KV_HEADS * HEAD_DIM
WINDOW = 128
N_EXPERTS = 8
LN_EPS = 1e-5
NEG_INF = -1e30

_IN_QKV = 2 * DN_QK_W + DN_V_W
_IN_Z = _IN_QKV + DN_V_W
_IN_B = _IN_Z + DN_HEADS
_IN_A = _IN_B + DN_HEADS
Q_OFF, K_OFF, V_OFF, Z_OFF = 0, DN_QK_W, 2 * DN_QK_W, _IN_QKV
SW_W = SWA_Q_W + 2 * SWA_KV_W

LANES = 128
SUBLANES = 8
VMEM_BYTES_V7X = 64 * 1024 * 1024
MIB = 1024 * 1024

CHUNK = 128


def _cparams(semantics, vmem_mib):
    assert vmem_mib * MIB < VMEM_BYTES_V7X, vmem_mib
    return pltpu.CompilerParams(dimension_semantics=semantics, vmem_limit_bytes=vmem_mib * MIB)


def _pick(n, pref):
    t = min(n, pref)
    while n % t:
        t //= 2
    return t


def _resident(block_shape, index_map):
    return pl.BlockSpec(block_shape, index_map, pipeline_mode=pl.Buffered(1))


def _mm_ws_body(*refs, n_parts, part_k):
    a_refs = refs[:n_parts]
    w_ref, o_ref, wb_ref = refs[n_parts:]

    @pl.when(pl.program_id(1) == 0)
    def _():
        wb_ref[...] = w_ref[...].astype(BF16)

    acc = jnp.dot(a_refs[0][...], wb_ref[0:part_k, :], preferred_element_type=F32)
    for p in range(1, n_parts):
        acc = acc + jnp.dot(a_refs[p][...], wb_ref[p * part_k:(p + 1) * part_k, :],
                            preferred_element_type=F32)
    o_ref[...] = acc.astype(o_ref.dtype)


def matmul_ws(a_parts, w, layer, out_dtype, *, n_cols=None, bm=512, bn=1024, name="mm"):
    m, part_k = a_parts[0].shape
    n_parts = len(a_parts)
    k = part_k * n_parts
    assert w.shape[1] == k
    n_cols = w.shape[2] if n_cols is None else n_cols
    bm, bn = _pick(m, bm), _pick(n_cols, bn)
    osz = jnp.dtype(out_dtype).itemsize
    vmem = k * bn * (4 + 2) + 2 * (bm * k * 2 + bm * bn * osz) + 2 * bm * bn * 4
    return pl.pallas_call(
        functools.partial(_mm_ws_body, n_parts=n_parts, part_k=part_k),
        grid=(n_cols // bn, m // bm),
        in_specs=[pl.BlockSpec((bm, part_k), lambda j, i: (i, 0)) for _ in range(n_parts)]
        + [_resident((None, k, bn), lambda j, i: (layer, 0, j))],
        out_specs=pl.BlockSpec((bm, bn), lambda j, i: (i, j)),
        out_shape=jax.ShapeDtypeStruct((m, n_cols), out_dtype),
        scratch_shapes=[pltpu.VMEM((k, bn), BF16)],
        compiler_params=_cparams(("arbitrary", "arbitrary"), vmem // MIB + 6),
        name=name,
    )(*a_parts, w)


def _in_proj_body(a_ref, wt_ref, o_ref, wb_ref):
    @pl.when(pl.program_id(1) == 0)
    def _():
        wb_ref[...] = wt_ref[...].astype(BF16)

    o_ref[...] = lax.dot_general(a_ref[...], wb_ref[...], (((1,), (1,)), ((), ())),
                                 preferred_element_type=F32).astype(o_ref.dtype)


def _in_proj_qkv_body(a_ref, wt_ref, cw_ref, o_ref, wb_ref, ext_ref, *, bm, bn, sub, piece, q_panels,
                      qk_panels):
    j = pl.program_id(0)
    i = pl.program_id(1)
    halo = SUBLANES

    @pl.when(i == 0)
    def _():
        wb_ref[...] = wt_ref[...].astype(BF16)
        ext_ref[0:halo, :] = jnp.zeros((halo, bn), F32)

    w = cw_ref[...]
    q_scale = jnp.where(j < q_panels, HEAD_DIM ** -0.5, 1.0)
    for r0 in range(0, bm, sub):
        ext_ref[halo + r0:halo + r0 + sub, :] = lax.dot_general(
            a_ref[r0:r0 + sub, :], wb_ref[...], (((1,), (1,)), ((), ())), preferred_element_type=F32)
        for p0 in range(r0, r0 + sub, piece):
            for h in range(bn // HEAD_DIM):
                hs = slice(h * HEAD_DIM, (h + 1) * HEAD_DIM)
                y = ext_ref[halo + p0:halo + p0 + piece, hs] * w[CONV_K - 1:CONV_K, hs]
                for tap in range(CONV_K - 1):
                    y = y + ext_ref[pl.ds(halo + p0 - (CONV_K - 1) + tap, piece), hs] * w[tap:tap + 1, hs]
                y = y * jax.nn.sigmoid(y)
                inv = lax.rsqrt(jnp.sum(y * y, -1, keepdims=True) + 1e-6) * q_scale
                o_ref[p0:p0 + piece, hs] = (y * jnp.where(j < qk_panels, inv, 1.0)).astype(o_ref.dtype)
    ext_ref[0:halo, :] = ext_ref[bm:bm + halo, :]


def in_proj(a, wt, layer, *, row0=0, n_cols=None, conv_w=None, bm=1024, bn=512, name="in_proj"):
    m, k = a.shape
    n_cols = wt.shape[1] if n_cols is None else n_cols
    bm, bn = _pick(m, bm), _pick(n_cols, bn)
    assert row0 % bn == 0
    vmem = 2 * bn * k * 4 + bn * k * 2 + 2 * (bm * k * 2 + bm * bn * 2) + bm * bn * 4
    a_spec = pl.BlockSpec((bm, k), lambda j, i: (i, 0))
    w_spec = pl.BlockSpec((None, bn, k), lambda j, i: (layer, row0 // bn + j, 0))
    common = dict(
        grid=(n_cols // bn, m // bm),
        out_specs=pl.BlockSpec((bm, bn), lambda j, i: (i, j)),
        out_shape=jax.ShapeDtypeStruct((m, n_cols), BF16),
        name=name)
    if conv_w is None:
        return pl.pallas_call(
            _in_proj_body, in_specs=[a_spec, w_spec],
            scratch_shapes=[pltpu.VMEM((bn, k), BF16)],
            compiler_params=_cparams(("arbitrary", "arbitrary"), vmem // MIB + 6), **common)(a, wt)
    assert n_cols == _IN_QKV and DN_QK_W % bn == 0 and bn % HEAD_DIM == 0
    vmem += (SUBLANES + bm) * bn * 4 + 2 * bm * bn * 4
    return pl.pallas_call(
        functools.partial(_in_proj_qkv_body, bm=bm, bn=bn, sub=_pick(bm, 256), piece=_pick(bm, 64),
                          q_panels=DN_QK_W // bn, qk_panels=2 * DN_QK_W // bn),
        in_specs=[a_spec, w_spec, pl.BlockSpec((None, CONV_K, bn), lambda j, i: (layer, 0, j))],
        scratch_shapes=[pltpu.VMEM((bn, k), BF16), pltpu.VMEM((SUBLANES + bm, bn), F32)],
        compiler_params=_cparams(("arbitrary", "arbitrary"), vmem // MIB + 6), **common)(a, wt, conv_w)


def _layer_norm_rows(r, g, b):
    mu = jnp.mean(r, -1, keepdims=True)
    c = r - mu
    var = jnp.mean(c * c, -1, keepdims=True)
    return c * lax.rsqrt(var + LN_EPS) * g + b


def _add_ln_body(x_ref, y_ref, g_ref, b_ref, of_ref, ob_ref, *, alpha):
    r = alpha * x_ref[...] + y_ref[...].astype(F32)
    o = _layer_norm_rows(r, g_ref[...], b_ref[...])
    of_ref[...] = o
    ob_ref[...] = o.astype(BF16)


def add_layer_norm(x, y, g, b, alpha, *, bm=256):
    t, d = x.shape
    bm = _pick(t, bm)
    row = pl.BlockSpec((bm, d), lambda i: (i, 0))
    vec = pl.BlockSpec((1, d), lambda i: (0, 0))
    vmem_mib = (2 * bm * d * (4 + y.dtype.itemsize + 4 + 2)) // MIB + 8
    return pl.pallas_call(
        functools.partial(_add_ln_body, alpha=alpha),
        grid=(t // bm,),
        in_specs=[row, row, vec, vec],
        out_specs=[row, row],
        out_shape=[jax.ShapeDtypeStruct((t, d), F32), jax.ShapeDtypeStruct((t, d), BF16)],
        compiler_params=_cparams(("parallel",), vmem_mib),
        name="add_ln",
    )(x, y, g.reshape(1, d), b.reshape(1, d))


def _swiglu_up_body(x_ref, w1_ref, w3_ref, h_ref, w1b_ref, w3b_ref):
    @pl.when(pl.program_id(1) == 0)
    def _():
        w1b_ref[...] = w1_ref[...].astype(BF16)
        w3b_ref[...] = w3_ref[...].astype(BF16)

    x = x_ref[...]
    a = jnp.dot(x, w1b_ref[...], preferred_element_type=F32)
    b = jnp.dot(x, w3b_ref[...], preferred_element_type=F32)
    h_ref[...] = (a * jax.nn.sigmoid(a) * b).astype(h_ref.dtype)


def swiglu_up(x, w1, w3, layer, *, bm=1024, bn=512):
    t, d = x.shape
    f = w1.shape[2]
    bm, bn = _pick(t, bm), _pick(f, bn)
    wspec = _resident((None, d, bn), lambda j, i: (layer, 0, j))
    vmem = 2 * d * bn * (4 + 2) + 2 * (bm * d * 2 + bm * bn * 2) + 4 * bm * bn * 4
    return pl.pallas_call(
        _swiglu_up_body,
        grid=(f // bn, t // bm),
        in_specs=[pl.BlockSpec((bm, d), lambda j, i: (i, 0)), wspec, wspec],
        out_specs=pl.BlockSpec((bm, bn), lambda j, i: (i, j)),
        out_shape=jax.ShapeDtypeStruct((t, f), BF16),
        scratch_shapes=[pltpu.VMEM((d, bn), BF16), pltpu.VMEM((d, bn), BF16)],
        compiler_params=_cparams(("arbitrary", "arbitrary"), vmem // MIB + 6),
        name="swiglu_up",
    )(x, w1, w3)


def _softplus(x):
    return jnp.maximum(x, 0.0) + jnp.log(1.0 + jnp.exp(-jnp.abs(x)))


def _bf16_terms(v):
    a = v.astype(BF16)
    r = v - a.astype(F32)
    b = r.astype(BF16)
    return a, b, (r - b.astype(F32)).astype(BF16)


def _dn_gate_body(x_ref, wbt_ref, wat_ref, alog_r_ref, dtb_r_ref, alog_c_ref, dtb_c_ref,
                  beta_ref, gam_ref, gamt_ref, w_ref, *, tb):
    x = x_ref[...]
    nt = (((1,), (1,)), ((), ()))

    @pl.when(pl.program_id(0) == 0)
    def _():
        w_ref[...] = jnp.concatenate([wbt_ref[...], wat_ref[...]], axis=0).astype(BF16)

    p = lax.dot_general(x, w_ref[...], nt, preferred_element_type=F32)
    beta_ref[...] = jax.nn.sigmoid(p[:, :LANES])
    pa = p[:, LANES:]
    g = -jnp.exp(alog_r_ref[...]) * _softplus(pa + dtb_r_ref[...])
    pt = pa.T[:DN_HEADS, :]
    gt = -jnp.exp(alog_c_ref[...]) * _softplus(pt + dtb_c_ref[...])
    r = lax.broadcasted_iota(I32, (tb, tb), 0)
    c = lax.broadcasted_iota(I32, (tb, tb), 1)
    same = (r // CHUNK) == (c // CHUNK)
    lower = jnp.where(same & (r >= c), 1.0, 0.0).astype(BF16)
    upper = jnp.where(same & (r <= c), 1.0, 0.0).astype(BF16)
    gam_ref[...] = sum(jnp.dot(lower, term, preferred_element_type=F32) for term in _bf16_terms(g))
    gamt_ref[...] = sum(jnp.dot(term, upper, preferred_element_type=F32) for term in _bf16_terms(gt))


def dn_gates(x, wbt, wat, a_log, dt_bias, *, tb=512):
    t, d = x.shape
    tb = _pick(t, tb)
    pad = lambda v: jnp.zeros((1, LANES), F32).at[0, :DN_HEADS].set(v.astype(F32))
    col = lambda v: v.astype(F32).reshape(DN_HEADS, 1)
    full = lambda shape: pl.BlockSpec(shape, lambda i: (0, 0))
    return pl.pallas_call(
        functools.partial(_dn_gate_body, tb=tb),
        grid=(t // tb,),
        in_specs=[pl.BlockSpec((tb, d), lambda i: (i, 0)),
                  full((LANES, d)), full((LANES, d)),
                  full((1, LANES)), full((1, LANES)), full((DN_HEADS, 1)), full((DN_HEADS, 1))],
        out_specs=[pl.BlockSpec((tb, LANES), lambda i: (i, 0)),
                   pl.BlockSpec((tb, LANES), lambda i: (i, 0)),
                   pl.BlockSpec((DN_HEADS, tb), lambda i: (0, i))],
        out_shape=[jax.ShapeDtypeStruct((t, LANES), F32),
                   jax.ShapeDtypeStruct((t, LANES), F32),
                   jax.ShapeDtypeStruct((DN_HEADS, t), F32)],
        scratch_shapes=[pltpu.VMEM((2 * LANES, d), BF16)],
        compiler_params=_cparams(("arbitrary",), 32),
        name="dn_gates",
    )(x, wbt, wat, pad(a_log), pad(dt_bias), col(a_log), col(dt_bias))


def _mm16(a, b):
    return jnp.dot(a.astype(BF16), b.astype(BF16), preferred_element_type=F32)


def _mm16_each(xs, ys):
    return [_mm16(x, y) for x, y in zip(xs, ys)]


def _unit_lower_inverse_each(a_list, eye, diag_blocks):
    d = [jnp.where(diag_blocks, a, 0.0) for a in a_list]
    n = [a - di for a, di in zip(a_list, d)]
    d2 = _mm16_each(d, d)
    x = _mm16_each([eye - t for t in d], [eye + t for t in d2])
    d4 = _mm16_each(d2, d2)
    x = _mm16_each(x, [eye + t for t in d4])
    d8 = _mm16_each(d4, d4)
    x = _mm16_each(x, [eye + t for t in d8])
    m = _mm16_each(x, n)
    m2 = _mm16_each(m, m)
    y = _mm16_each([eye - t for t in m], [eye + t for t in m2])
    m4 = _mm16_each(m2, m2)
    y = _mm16_each(y, [eye + t for t in m4])
    return _mm16_each(y, x)


def _delta_body(q_ref, k_ref, v_ref, z_ref, beta_ref, gam_ref, gamt_ref, nw_ref, o_ref, s_ref, *,
                hb, tb):
    hg = pl.program_id(0)
    t = pl.program_id(1)

    @pl.when(t == 0)
    def _():
        s_ref[...] = jnp.zeros_like(s_ref)

    row = lax.broadcasted_iota(I32, (CHUNK, CHUNK), 0)
    col = lax.broadcasted_iota(I32, (CHUNK, CHUNK), 1)
    causal = row >= col
    strict = row > col
    diag_blocks = (row // 16) == (col // 16)
    eye = jnp.where(row == col, 1.0, 0.0).astype(F32)
    lane = lax.broadcasted_iota(I32, (tb, LANES), 1)
    nw = nw_ref[...]
    heads = range(hb)
    sq = (CHUNK, CHUNK)

    bcol, gcol, grow = [], [], []
    for hl in heads:
        head = hg * hb + hl
        sel = lane == head
        bcol.append(jnp.sum(jnp.where(sel, beta_ref[...], 0.0), axis=-1, keepdims=True))
        gcol.append(jnp.sum(jnp.where(sel, gam_ref[...], 0.0), axis=-1, keepdims=True))
        grow.append(gamt_ref[pl.ds(head, 1), :])

    for c in range(tb // CHUNK):
        rs = slice(c * CHUNK, (c + 1) * CHUNK)
        cs = [slice(hl * HEAD_DIM, (hl + 1) * HEAD_DIM) for hl in heads]
        q = [q_ref[rs, cs[h]].astype(F32) for h in heads]
        k = [k_ref[rs, cs[h]].astype(F32) for h in heads]
        v = [v_ref[rs, cs[h]].astype(F32) for h in heads]
        bc = [jnp.broadcast_to(bcol[h][rs, :], sq) for h in heads]
        gc = [jnp.broadcast_to(gcol[h][rs, :], sq) for h in heads]
        gr = [jnp.broadcast_to(grow[h][:, rs], sq) for h in heads]
        g_last = [jnp.broadcast_to(grow[h][:, (c + 1) * CHUNK - 1:(c + 1) * CHUNK], sq) for h in heads]
        decay = [jnp.exp(jnp.where(causal, gc[h] - gr[h], NEG_INF)) for h in heads]
        egc = [jnp.exp(gc[h]) for h in heads]
        kt = [k[h].T for h in heads]
        gram = _mm16_each([jnp.concatenate([q[h], k[h]], axis=0) for h in heads], kt)
        a_qk = [gram[h][:CHUNK] * decay[h] for h in heads]
        a_kk = [jnp.where(strict, bc[h] * gram[h][CHUNK:] * decay[h], 0.0) for h in heads]
        tinv = _unit_lower_inverse_each(a_kk, eye, diag_blocks)
        sol = _mm16_each(tinv, [jnp.concatenate([v[h] * bc[h], k[h] * (bc[h] * egc[h])], axis=1)
                                for h in heads])
        lhs1 = [jnp.concatenate([sol[h][:, HEAD_DIM:], q[h] * egc[h]], axis=0) for h in heads]
        kdt = [kt[h] * jnp.exp(g_last[h] - gr[h]) for h in heads]
        s = [s_ref[h] for h in heads]
        ws_qs = _mm16_each(lhs1, s)
        v_new = [sol[h][:, :HEAD_DIM] - ws_qs[h][:CHUNK] for h in heads]
        av_ds = _mm16_each([jnp.concatenate([a_qk[h], kdt[h]], axis=0) for h in heads], v_new)
        for h in heads:
            s_ref[h] = s[h] * jnp.exp(g_last[h]) + av_ds[h][CHUNK:]
            o = ws_qs[h][CHUNK:] + av_ds[h][:CHUNK]
            o = o * lax.rsqrt(jnp.mean(o * o, -1, keepdims=True) + 1e-6) * nw
            zz = z_ref[rs, cs[h]].astype(F32)
            o_ref[rs, cs[h]] = (o * (zz * jax.nn.sigmoid(zz))).astype(o_ref.dtype)


def delta_net(qkv, z, beta, gam, gamt, norm_w, *, hb=8, tb=512):
    t = qkv.shape[0]
    tb = _pick(t, tb)
    w = hb * HEAD_DIM
    pspec = lambda off: pl.BlockSpec((tb, w), lambda h, i: (i, off // w + h))
    return pl.pallas_call(
        functools.partial(_delta_body, hb=hb, tb=tb),
        grid=(DN_HEADS // hb, t // tb),
        in_specs=[pspec(Q_OFF), pspec(K_OFF), pspec(V_OFF), pspec(0),
                  pl.BlockSpec((tb, LANES), lambda h, i: (i, 0)),
                  pl.BlockSpec((tb, LANES), lambda h, i: (i, 0)),
                  pl.BlockSpec((DN_HEADS, tb), lambda h, i: (0, i)),
                  pl.BlockSpec((1, HEAD_DIM), lambda h, i: (0, 0))],
        out_specs=pl.BlockSpec((tb, w), lambda h, i: (i, h)),
        out_shape=jax.ShapeDtypeStruct((t, DN_V_W), BF16),
        scratch_shapes=[pltpu.VMEM((hb, HEAD_DIM, HEAD_DIM), F32)],
        compiler_params=_cparams(("arbitrary", "arbitrary"), 48),
        name="delta_net",
    )(qkv, qkv, qkv, z, beta, gam, gamt, norm_w.astype(F32).reshape(1, HEAD_DIM))


_SWA_QB = 4


def _swa_body(sink_ref, q_ref, kc_ref, kp_ref, vc_ref, vp_ref, o_ref):
    i = pl.program_id(0)
    qi = lax.broadcasted_iota(I32, (WINDOW, 2 * WINDOW), 0)
    kj = lax.broadcasted_iota(I32, (WINDOW, 2 * WINDOW), 1)
    dist = qi - kj + WINDOW
    in_window = (dist >= 0) & (dist < WINDOW)
    first_key = jnp.where(i > 0, 0, WINDOW)
    valid_first = in_window & (kj >= first_key)
    distf = dist.astype(F32)
    scale = HEAD_DIM ** -0.5
    for h in range(SWA_KV_HEADS):
        hs = slice(h * HEAD_DIM, (h + 1) * HEAD_DIM)
        for b in range(_SWA_QB):
            rows = slice(b * WINDOW, (b + 1) * WINDOW)
            if b == 0:
                kk = jnp.concatenate([kp_ref[:, hs], kc_ref[rows, hs]], axis=0)
                vv = jnp.concatenate([vp_ref[:, hs], vc_ref[rows, hs]], axis=0)
                valid = valid_first
            else:
                kk = kc_ref[(b - 1) * WINDOW:(b + 1) * WINDOW, hs]
                vv = vc_ref[(b - 1) * WINDOW:(b + 1) * WINDOW, hs]
                valid = in_window
            vv1 = jnp.concatenate([vv, jnp.ones_like(vv)], axis=1)
            for g in range(SWA_GROUP):
                hq = h * SWA_GROUP + g
                slope = 2.0 ** (-8.0 * (hq + 1) / SWA_Q_HEADS)
                qs = slice(hq * HEAD_DIM, (hq + 1) * HEAD_DIM)
                s = lax.dot_general(q_ref[rows, qs], kk, (((1,), (1,)), ((), ())),
                                    preferred_element_type=F32) * scale
                logits = jnp.where(valid, s - slope * distf, NEG_INF)
                sink = sink_ref[hq]
                m = jnp.maximum(jnp.max(logits, -1, keepdims=True), sink)
                e = jnp.exp(logits - m).astype(BF16)
                ov = jnp.dot(e, vv1, preferred_element_type=F32)
                denom = ov[:, HEAD_DIM:] + jnp.exp(sink - m)
                o_ref[rows, qs] = (ov[:, :HEAD_DIM] / denom).astype(o_ref.dtype)


def swa_attention(proj_sw, sinks):
    t = proj_sw.shape[0]
    qrows = _SWA_QB * WINDOW
    assert t % qrows == 0
    kb, vb = SWA_Q_W // SWA_KV_W, SWA_Q_W // SWA_KV_W + 1
    cur = lambda blk: pl.BlockSpec((qrows, SWA_KV_W), lambda i: (i, blk))
    prev = lambda blk: pl.BlockSpec((WINDOW, SWA_KV_W), lambda i: (jnp.maximum(_SWA_QB * i - 1, 0), blk))
    return pl.pallas_call(
        _swa_body,
        grid=(t // qrows,),
        in_specs=[pl.BlockSpec(memory_space=pltpu.SMEM),
                  pl.BlockSpec((qrows, SWA_Q_W), lambda i: (i, 0)),
                  cur(kb), prev(kb), cur(vb), prev(vb)],
        out_specs=pl.BlockSpec((qrows, SWA_Q_W), lambda i: (i, 0)),
        out_shape=jax.ShapeDtypeStruct((t, SWA_Q_W), BF16),
        compiler_params=_cparams(("parallel",), 32),
        name="swa",
    )(sinks.astype(F32), proj_sw, proj_sw, proj_sw, proj_sw, proj_sw)


def _top2_gates(x, w):
    logits = jnp.dot(x, w, precision=lax.Precision.HIGHEST, preferred_element_type=F32)
    lane = lax.broadcasted_iota(I32, logits.shape, 1)
    neg = jnp.float32(-jnp.inf)
    l1 = jnp.where(lane < N_EXPERTS, logits, neg)
    m1 = jnp.max(l1, -1, keepdims=True)
    i1 = jnp.min(jnp.where(l1 == m1, lane, LANES), -1, keepdims=True)
    l2 = jnp.where(lane == i1, neg, l1)
    m2 = jnp.max(l2, -1, keepdims=True)
    i2 = jnp.min(jnp.where(l2 == m2, lane, LANES), -1, keepdims=True)
    e = jnp.exp(m2 - m1)
    w1 = 1.0 / (1.0 + e)
    w2 = e * w1
    idx = jnp.where(lane == 0, i1, jnp.where(lane == 1, i2, 0))
    gates = jnp.where(lane == 0, w1, jnp.where(lane == 1, w2, 0.0))
    return idx, gates


def _add_ln_router_body(x_ref, y_ref, g_ref, b_ref, rw_ref, of_ref, idx_ref, gate_ref, *, alpha):
    r = alpha * x_ref[...] + y_ref[...].astype(F32)
    o = _layer_norm_rows(r, g_ref[...], b_ref[...])
    of_ref[...] = o
    idx_ref[...], gate_ref[...] = _top2_gates(o, rw_ref[...])


def add_layer_norm_router(x, y, g, b, alpha, router_w, *, bm=256):
    t, d = x.shape
    bm = _pick(t, bm)
    w = jnp.zeros((d, LANES), F32).at[:, :N_EXPERTS].set(router_w.astype(F32))
    row = lambda width: pl.BlockSpec((bm, width), lambda i: (i, 0))
    vec = pl.BlockSpec((1, d), lambda i: (0, 0))
    vmem_mib = (2 * bm * d * (4 + y.dtype.itemsize + 4) + 2 * d * LANES * 4 + 6 * bm * d * 4) // MIB + 8
    return pl.pallas_call(
        functools.partial(_add_ln_router_body, alpha=alpha),
        grid=(t // bm,),
        in_specs=[row(d), row(d), vec, vec, pl.BlockSpec((d, LANES), lambda i: (0, 0))],
        out_specs=[row(d), row(LANES), row(LANES)],
        out_shape=[jax.ShapeDtypeStruct((t, d), F32),
                   jax.ShapeDtypeStruct((t, LANES), I32), jax.ShapeDtypeStruct((t, LANES), F32)],
        compiler_params=_cparams(("parallel",), vmem_mib),
        name="add_ln_router",
    )(x, y, g.reshape(1, d), b.reshape(1, d), w)


def _route_plan(top_i, tm):
    t = top_i.shape[0]
    n_assign = 2 * t
    e_flat = top_i.reshape(-1)
    onehot = (e_flat[:, None] == jnp.arange(N_EXPERTS, dtype=I32)[None, :]).astype(I32)
    csum = jnp.cumsum(onehot, axis=0)
    rank = jnp.sum(csum * onehot, axis=1) - 1
    counts = csum[-1]
    padded = ((counts + tm - 1) // tm) * tm
    ends = jnp.cumsum(padded)
    offs = ends - padded
    pos = offs[e_flat] + rank
    n_tiles = n_assign // tm + N_EXPERTS
    row_src = jnp.zeros((n_tiles * tm,), I32).at[pos].set(jnp.arange(n_assign, dtype=I32) // 2)
    tile_start = jnp.arange(n_tiles, dtype=I32) * tm
    tile_e = jnp.sum((tile_start[:, None] >= ends[None, :]).astype(I32), axis=1)
    tile_valid = (tile_start < ends[-1]).astype(I32)
    last_e = jnp.max(jnp.where(padded > 0, jnp.arange(N_EXPERTS, dtype=I32), 0))
    tile_e = jnp.minimum(tile_e, last_e)
    return row_src, pos[0::2], pos[1::2], tile_e, tile_valid, n_tiles


_ISSUE_UNROLL = 8


def _gather_rows_body(src_ref, x_hbm, o_ref, buf, sem, *, rows):
    i = pl.program_id(0)
    slot = lax.rem(i, 2)

    def row_copy(step_slot, r, src_row):
        return pltpu.make_async_copy(x_hbm.at[pl.ds(src_row, 1), :],
                                     buf.at[step_slot, pl.ds(r, 1), :], sem.at[step_slot])

    def issue(step, step_slot):
        def body(r, carry):
            row_copy(step_slot, r, src_ref[step * rows + r]).start()
            return carry
        lax.fori_loop(0, rows, body, 0, unroll=_ISSUE_UNROLL)

    @pl.when(i == 0)
    def _():
        issue(0, 0)

    @pl.when(i + 1 < pl.num_programs(0))
    def _():
        issue(i + 1, 1 - slot)

    def drain(r, carry):
        row_copy(slot, r, 0).wait()
        return carry

    lax.fori_loop(0, rows, drain, 0, unroll=_ISSUE_UNROLL)
    o_ref[...] = buf[slot].astype(o_ref.dtype)


def gather_rows(x, row_src, *, rows=512):
    d = x.shape[1]
    p = row_src.shape[0]
    rows = _pick(p, rows)
    return pl.pallas_call(
        functools.partial(_gather_rows_body, rows=rows),
        grid_spec=pltpu.PrefetchScalarGridSpec(
            num_scalar_prefetch=1,
            grid=(p // rows,),
            in_specs=[pl.BlockSpec(memory_space=pl.ANY)],
            out_specs=pl.BlockSpec((rows, d), lambda i, src: (i, 0)),
            scratch_shapes=[pltpu.VMEM((2, rows, d), F32), pltpu.SemaphoreType.DMA((2,))]),
        out_shape=jax.ShapeDtypeStruct((p, d), BF16),
        compiler_params=_cparams(("arbitrary",), (rows * d * (2 * 4 + 2 * 2 + 4)) // MIB + 6),
        name="moe_gather",
    )(row_src, x)


def _new_panel(te_ref, i):
    return (i == 0) | (te_ref[i] != te_ref[jnp.maximum(i - 1, 0)])


def _expert_up_body(te_ref, tv_ref, x_ref, w1_ref, w3_ref, h_ref, w1b_ref, w3b_ref):
    i = pl.program_id(1)

    @pl.when(_new_panel(te_ref, i))
    def _():
        w1b_ref[...] = w1_ref[...].astype(BF16)
        w3b_ref[...] = w3_ref[...].astype(BF16)

    @pl.when(tv_ref[i] == 1)
    def _():
        x = x_ref[...]
        a = jnp.dot(x, w1b_ref[...], preferred_element_type=F32)
        b = jnp.dot(x, w3b_ref[...], preferred_element_type=F32)
        h_ref[...] = (a * jax.nn.sigmoid(a) * b).astype(h_ref.dtype)

    @pl.when(tv_ref[i] == 0)
    def _():
        h_ref[...] = jnp.zeros_like(h_ref)


def expert_up(xs, w1, w3, layer, tile_e, tile_valid, tm, *, bn=512):
    p, d = xs.shape
    f = w1.shape[3]
    bn = _pick(f, bn)
    wspec = pl.BlockSpec((None, None, d, bn), lambda j, i, te, tv: (layer, te[i], 0, j))
    vmem = 2 * d * bn * (2 * 4 + 2) + 2 * (tm * d * 2 + tm * bn * 2) + 3 * tm * bn * 4
    return pl.pallas_call(
        _expert_up_body,
        grid_spec=pltpu.PrefetchScalarGridSpec(
            num_scalar_prefetch=2,
            grid=(f // bn, p // tm),
            in_specs=[pl.BlockSpec((tm, d), lambda j, i, te, tv: (i, 0)), wspec, wspec],
            out_specs=pl.BlockSpec((tm, bn), lambda j, i, te, tv: (i, j)),
            scratch_shapes=[pltpu.VMEM((d, bn), BF16), pltpu.VMEM((d, bn), BF16)]),
        out_shape=jax.ShapeDtypeStruct((p, f), BF16),
        compiler_params=_cparams(("arbitrary", "arbitrary"), vmem // MIB + 6),
        name="expert_up",
    )(tile_e, tile_valid, xs, w1, w3)


def _expert_down_body(te_ref, tv_ref, h_ref, w2_ref, y_ref, w2b_ref):
    i = pl.program_id(1)

    @pl.when(_new_panel(te_ref, i))
    def _():
        w2b_ref[...] = w2_ref[...].astype(BF16)

    @pl.when(tv_ref[i] == 1)
    def _():
        y_ref[...] = jnp.dot(h_ref[...], w2b_ref[...], preferred_element_type=F32).astype(y_ref.dtype)

    @pl.when(tv_ref[i] == 0)
    def _():
        y_ref[...] = jnp.zeros_like(y_ref)


def expert_down(h, w2, layer, tile_e, tile_valid, tm, *, bn=1024):
    p, f = h.shape
    d = w2.shape[3]
    bn = _pick(d, bn)
    vmem = f * bn * (2 * 4 + 2) + 2 * (tm * f * 2 + tm * bn * 4) + 2 * tm * bn * 4
    return pl.pallas_call(
        _expert_down_body,
        grid_spec=pltpu.PrefetchScalarGridSpec(
            num_scalar_prefetch=2,
            grid=(d // bn, p // tm),
            in_specs=[pl.BlockSpec((tm, f), lambda j, i, te, tv: (i, 0)),
                      pl.BlockSpec((None, None, f, bn), lambda j, i, te, tv: (layer, te[i], 0, j))],
            out_specs=pl.BlockSpec((tm, bn), lambda j, i, te, tv: (i, j)),
            scratch_shapes=[pltpu.VMEM((f, bn), BF16)]),
        out_shape=jax.ShapeDtypeStruct((p, d), F32),
        compiler_params=_cparams(("arbitrary", "arbitrary"), vmem // MIB + 6),
        name="expert_down",
    )(tile_e, tile_valid, h, w2)


def _combine_ln_body(p1_ref, p2_ref, y_hbm, x_ref, gate_ref, g_ref, b_ref, of_ref, ob_ref,
                     buf, sem, *, rows, alpha):
    i = pl.program_id(0)
    slot = lax.rem(i, 2)

    def row_copy(step_slot, which, r, src_row):
        return pltpu.make_async_copy(y_hbm.at[pl.ds(src_row, 1), :],
                                     buf.at[step_slot, which, pl.ds(r, 1), :], sem.at[step_slot])

    def issue(step, step_slot):
        def body(r, carry):
            row_copy(step_slot, 0, r, p1_ref[step * rows + r]).start()
            row_copy(step_slot, 1, r, p2_ref[step * rows + r]).start()
            return carry
        lax.fori_loop(0, rows, body, 0, unroll=_ISSUE_UNROLL)

    @pl.when(i == 0)
    def _():
        issue(0, 0)

    @pl.when(i + 1 < pl.num_programs(0))
    def _():
        issue(i + 1, 1 - slot)

    def drain(r, carry):
        row_copy(slot, 0, r, 0).wait()
        row_copy(slot, 1, r, 0).wait()
        return carry

    lax.fori_loop(0, rows, drain, 0, unroll=_ISSUE_UNROLL)
    gates = gate_ref[...]
    f = gates[:, 0:1] * buf[slot, 0] + gates[:, 1:2] * buf[slot, 1]
    o = _layer_norm_rows(alpha * x_ref[...] + f, g_ref[...], b_ref[...])
    of_ref[...] = o
    ob_ref[...] = o.astype(BF16)


def moe_combine_ln(y, pos1, pos2, gates, x, g, b, alpha, *, rows=128):
    t, d = x.shape
    rows = _pick(t, rows)
    row = lambda width: pl.BlockSpec((rows, width), lambda i, p1, p2: (i, 0))
    vec = pl.BlockSpec((1, d), lambda i, p1, p2: (0, 0))
    return pl.pallas_call(
        functools.partial(_combine_ln_body, rows=rows, alpha=alpha),
        grid_spec=pltpu.PrefetchScalarGridSpec(
            num_scalar_prefetch=2,
            grid=(t // rows,),
            in_specs=[pl.BlockSpec(memory_space=pl.ANY), row(d), row(LANES), vec, vec],
            out_specs=[row(d), row(d)],
            scratch_shapes=[pltpu.VMEM((2, 2, rows, d), F32), pltpu.SemaphoreType.DMA((2,))]),
        out_shape=[jax.ShapeDtypeStruct((t, d), F32), jax.ShapeDtypeStruct((t, d), BF16)],
        compiler_params=_cparams(("arbitrary",), (rows * d * (16 + 2 * (4 + 4 + 2) + 8)) // MIB + 6),
        name="moe_combine_ln",
    )(pos1, pos2, y, x, gates, g.reshape(1, d), b.reshape(1, d))


def moe_ffn_ln(x_f32, idx, gates, w1, w3, w2, layer, g, b, alpha, *, tm=512):
    row_src, pos1, pos2, tile_e, tile_valid, _ = _route_plan(idx[:, :2], tm)
    xs = gather_rows(x_f32, row_src, rows=tm)
    h = expert_up(xs, w1, w3, layer, tile_e, tile_valid, tm)
    y = expert_down(h, w2, layer, tile_e, tile_valid, tm)
    return moe_combine_ln(y, pos1, pos2, gates, x_f32, g, b, alpha)


def _gate_weights(wt_in, layer):
    d = wt_in.shape[2]
    pad = lambda rows: jnp.zeros((LANES, d), F32).at[:DN_HEADS].set(rows)
    return pad(wt_in[layer, _IN_Z:_IN_B, :]), pad(wt_in[layer, _IN_B:_IN_A, :])


def kernel(x, w_in, conv_w, a_log, dt_bias, dn_norm_w, sinks, w_o, ln1_g, ln1_b, ffn_w1, ffn_w3,
           ffn_w2, router_w, exp_w1, exp_w3, exp_w2, ln2_g, ln2_b):
    bsz, seq, d = x.shape
    depth = w_in.shape[0]
    alpha = (2 * depth) ** 0.25
    w_in, w_o, conv_w = w_in.astype(F32), w_o.astype(F32), conv_w.astype(F32)
    wt_in = jnp.swapaxes(w_in, 1, 2)
    wt_sw = wt_in[:, _IN_A:, :]
    outs = []
    for bi in range(bsz):
        xf = x[bi].astype(F32)
        xb = xf.astype(BF16)
        for i in range(depth):
            w_bt, w_at = _gate_weights(wt_in, i)
            qkv = in_proj(xb, wt_in, i, n_cols=_IN_QKV, conv_w=conv_w, name="in_proj_qkv")
            z = in_proj(xb, wt_in, i, row0=_IN_QKV, n_cols=DN_V_W, name="in_proj_z")
            proj_sw = in_proj(xb, wt_sw, i, name="in_proj_sw")
            beta, gam, gamt = dn_gates(xb, w_bt, w_at, a_log[i], dt_bias[i])
            out_a = delta_net(qkv, z, beta, gam, gamt, dn_norm_w[i])
            out_b = swa_attention(proj_sw, sinks[i])
            mix = matmul_ws([out_a, out_b], w_o, i, BF16, name="out_proj")
            j = i // 2
            if i % 2 == 0:
                xf, xb = add_layer_norm(xf, mix, ln1_g[i], ln1_b[i], alpha)
                h = swiglu_up(xb, ffn_w1.astype(F32), ffn_w3.astype(F32), j)
                f = matmul_ws([h], ffn_w2.astype(F32), j, BF16, bm=512, bn=512, name="ffn_down")
                xf, xb = add_layer_norm(xf, f, ln2_g[i], ln2_b[i], alpha)
            else:
                xf, idx, gates = add_layer_norm_router(xf, mix, ln1_g[i], ln1_b[i], alpha, router_w[j])
                xf, xb = moe_ffn_ln(xf, idx, gates, exp_w1.astype(F32), exp_w3.astype(F32),
                                    exp_w2.astype(F32), j, ln2_g[i], ln2_b[i], alpha)
        outs.append(xf)
    return jnp.stack(outs, axis=0).astype(x.dtype)
```

```python
import functools

import jax
import jax.numpy as jnp
from jax import lax
from jax.experimental import pallas as pl
from jax.experimental.pallas import tpu as pltpu

F32 = jnp.float32
BF16 = jnp.bfloat16
I32 = jnp.int32

HEAD_DIM = 128
DN_HEADS = 16
DN_QK_W = DN_HEADS * HEAD_DIM
DN_V_W = DN_HEADS * HEAD_DIM
CONV_K = 4
SWA_Q_HEADS = 16
SWA_KV_HEADS = 4
SWA_GROUP = SWA_Q_HEADS // SWA_KV_HEADS
SWA_Q_W = SWA_Q_HEADS * HEAD_DIM
SWA_KV_W = SWA_KV_HEADS * HEAD_DIM
WINDOW = 128
N_EXPERTS = 8
LN_EPS = 1e-5
NEG_INF = -1e30

_IN_QKV = 2 * DN_QK_W + DN_V_W
_IN_Z = _IN_QKV + DN_V_W
_IN_B = _IN_Z + DN_HEADS
_IN_A = _IN_B + DN_HEADS
Q_OFF, K_OFF, V_OFF, Z_OFF = 0, DN_QK_W, 2 * DN_QK_W, _IN_QKV
SW_W = SWA_Q_W + 2 * SWA_KV_W

LANES = 128
SUBLANES = 8
VMEM_BYTES_V7X = 64 * 1024 * 1024
MIB = 1024 * 1024

CHUNK = 128


def _cparams(semantics, vmem_mib):
    assert vmem_mib * MIB < VMEM_BYTES_V7X, vmem_mib
    return pltpu.CompilerParams(dimension_semantics=semantics, vmem_limit_bytes=vmem_mib * MIB)


def _pick(n, pref):
    t = min(n, pref)
    while n % t:
        t //= 2
    return t


def _resident(block_shape, index_map):
    return pl.BlockSpec(block_shape, index_map, pipeline_mode=pl.Buffered(1))


def _mm_ws_body(*refs, n_parts, part_k):
    a_refs = refs[:n_parts]
    w_ref, o_ref, wb_ref = refs[n_parts:]

    @pl.when(pl.program_id(1) == 0)
    def _():
        wb_ref[...] = w_ref[...].astype(BF16)

    acc = jnp.dot(a_refs[0][...], wb_ref[0:part_k, :], preferred_element_type=F32)
    for p in range(1, n_parts):
        acc = acc + jnp.dot(a_refs[p][...], wb_ref[p * part_k:(p + 1) * part_k, :],
                            preferred_element_type=F32)
    o_ref[...] = acc.astype(o_ref.dtype)


def matmul_ws(a_parts, w, layer, out_dtype, *, n_cols=None, bm=512, bn=1024, name="mm"):
    m, part_k = a_parts[0].shape
    n_parts = len(a_parts)
    k = part_k * n_parts
    assert w.shape[1] == k
    n_cols = w.shape[2] if n_cols is None else n_cols
    bm, bn = _pick(m, bm), _pick(n_cols, bn)
    osz = jnp.dtype(out_dtype).itemsize
    vmem = k * bn * (4 + 2) + 2 * (bm * k * 2 + bm * bn * osz) + 2 * bm * bn * 4
    return pl.pallas_call(
        functools.partial(_mm_ws_body, n_parts=n_parts, part_k=part_k),
        grid=(n_cols // bn, m // bm),
        in_specs=[pl.BlockSpec((bm, part_k), lambda j, i: (i, 0)) for _ in range(n_parts)]
        + [_resident((None, k, bn), lambda j, i: (layer, 0, j))],
        out_specs=pl.BlockSpec((bm, bn), lambda j, i: (i, j)),
        out_shape=jax.ShapeDtypeStruct((m, n_cols), out_dtype),
        scratch_shapes=[pltpu.VMEM((k, bn), BF16)],
        compiler_params=_cparams(("arbitrary", "arbitrary"), vmem // MIB + 6),
        name=name,
    )(*a_parts, w)


def _in_proj_body(a_ref, wt_ref, o_ref, wb_ref):
    @pl.when(pl.program_id(1) == 0)
    def _():
        wb_ref[...] = wt_ref[...].astype(BF16)

    o_ref[...] = lax.dot_general(a_ref[...], wb_ref[...], (((1,), (1,)), ((), ())),
                                 preferred_element_type=F32).astype(o_ref.dtype)


def _in_proj_qkv_body(a_ref, wt_ref, cw_ref, o_ref, wb_ref, ext_ref, *, bm, bn, sub, piece, q_panels,
                      qk_panels):
    j = pl.program_id(0)
    i = pl.program_id(1)
    halo = SUBLANES

    @pl.when(i == 0)
    def _():
        wb_ref[...] = wt_ref[...].astype(BF16)
        ext_ref[0:halo, :] = jnp.zeros((halo, bn), F32)

    w = cw_ref[...]
    q_scale = jnp.where(j < q_panels, HEAD_DIM ** -0.5, 1.0)
    for r0 in range(0, bm, sub):
        ext_ref[halo + r0:halo + r0 + sub, :] = lax.dot_general(
            a_ref[r0:r0 + sub, :], wb_ref[...], (((1,), (1,)), ((), ())), preferred_element_type=F32)
        for p0 in range(r0, r0 + sub, piece):
            for h in range(bn // HEAD_DIM):
                hs = slice(h * HEAD_DIM, (h + 1) * HEAD_DIM)
                y = ext_ref[halo + p0:halo + p0 + piece, hs] * w[CONV_K - 1:CONV_K, hs]
                for tap in range(CONV_K - 1):
                    y = y + ext_ref[pl.ds(halo + p0 - (CONV_K - 1) + tap, piece), hs] * w[tap:tap + 1, hs]
                y = y * jax.nn.sigmoid(y)
                inv = lax.rsqrt(jnp.sum(y * y, -1, keepdims=True) + 1e-6) * q_scale
                o_ref[p0:p0 + piece, hs] = (y * jnp.where(j < qk_panels, inv, 1.0)).astype(o_ref.dtype)
    ext_ref[0:halo, :] = ext_ref[bm:bm + halo, :]


def in_proj(a, wt, layer, *, row0=0, n_cols=None, conv_w=None, bm=1024, bn=512, name="in_proj"):
    m, k = a.shape
    n_cols = wt.shape[1] if n_cols is None else n_cols
    bm, bn = _pick(m, bm), _pick(n_cols, bn)
    assert row0 % bn == 0
    vmem = 2 * bn * k * 4 + bn * k * 2 + 2 * (bm * k * 2 + bm * bn * 2) + bm * bn * 4
    a_spec = pl.BlockSpec((bm, k), lambda j, i: (i, 0))
    w_spec = pl.BlockSpec((None, bn, k), lambda j, i: (layer, row0 // bn + j, 0))
    common = dict(
        grid=(n_cols // bn, m // bm),
        out_specs=pl.BlockSpec((bm, bn), lambda j, i: (i, j)),
        out_shape=jax.ShapeDtypeStruct((m, n_cols), BF16),
        name=name)
    if conv_w is None:
        return pl.pallas_call(
            _in_proj_body, in_specs=[a_spec, w_spec],
            scratch_shapes=[pltpu.VMEM((bn, k), BF16)],
            compiler_params=_cparams(("arbitrary", "arbitrary"), vmem // MIB + 6), **common)(a, wt)
    assert n_cols == _IN_QKV and DN_QK_W % bn == 0 and bn % HEAD_DIM == 0
    vmem += (SUBLANES + bm) * bn * 4 + 2 * bm * bn * 4
    return pl.pallas_call(
        functools.partial(_in_proj_qkv_body, bm=bm, bn=bn, sub=_pick(bm, 256), piece=_pick(bm, 64),
                          q_panels=DN_QK_W // bn, qk_panels=2 * DN_QK_W // bn),
        in_specs=[a_spec, w_spec, pl.BlockSpec((None, CONV_K, bn), lambda j, i: (layer, 0, j))],
        scratch_shapes=[pltpu.VMEM((bn, k), BF16), pltpu.VMEM((SUBLANES + bm, bn), F32)],
        compiler_params=_cparams(("arbitrary", "arbitrary"), vmem // MIB + 6), **common)(a, wt, conv_w)


def _layer_norm_rows(r, g, b):
    mu = jnp.mean(r, -1, keepdims=True)
    c = r - mu
    var = jnp.mean(c * c, -1, keepdims=True)
    return c * lax.rsqrt(var + LN_EPS) * g + b


def _add_ln_body(x_ref, y_ref, g_ref, b_ref, of_ref, ob_ref, *, alpha):
    r = alpha * x_ref[...] + y_ref[...].astype(F32)
    o = _layer_norm_rows(r, g_ref[...], b_ref[...])
    of_ref[...] = o
    ob_ref[...] = o.astype(BF16)


def add_layer_norm(x, y, g, b, alpha, *, bm=256):
    t, d = x.shape
    bm = _pick(t, bm)
    row = pl.BlockSpec((bm, d), lambda i: (i, 0))
    vec = pl.BlockSpec((1, d), lambda i: (0, 0))
    vmem_mib = (2 * bm * d * (4 + y.dtype.itemsize + 4 + 2)) // MIB + 8
    return pl.pallas_call(
        functools.partial(_add_ln_body, alpha=alpha),
        grid=(t // bm,),
        in_specs=[row, row, vec, vec],
        out_specs=[row, row],
        out_shape=[jax.ShapeDtypeStruct((t, d), F32), jax.ShapeDtypeStruct((t, d), BF16)],
        compiler_params=_cparams(("parallel",), vmem_mib),
        name="add_ln",
    )(x, y, g.reshape(1, d), b.reshape(1, d))


def _swiglu_up_body(x_ref, w1_ref, w3_ref, h_ref, w1b_ref, w3b_ref):
    @pl.when(pl.program_id(1) == 0)
    def _():
        w1b_ref[...] = w1_ref[...].astype(BF16)
        w3b_ref[...] = w3_ref[...].astype(BF16)

    x = x_ref[...]
    a = jnp.dot(x, w1b_ref[...], preferred_element_type=F32)
    b = jnp.dot(x, w3b_ref[...], preferred_element_type=F32)
    h_ref[...] = (a * jax.nn.sigmoid(a) * b).astype(h_ref.dtype)


def swiglu_up(x, w1, w3, layer, *, bm=1024, bn=512):
    t, d = x.shape
    f = w1.shape[2]
    bm, bn = _pick(t, bm), _pick(f, bn)
    wspec = _resident((None, d, bn), lambda j, i: (layer, 0, j))
    vmem = 2 * d * bn * (4 + 2) + 2 * (bm * d * 2 + bm * bn * 2) + 4 * bm * bn * 4
    return pl.pallas_call(
        _swiglu_up_body,
        grid=(f // bn, t // bm),
        in_specs=[pl.BlockSpec((bm, d), lambda j, i: (i, 0)), wspec, wspec],
        out_specs=pl.BlockSpec((bm, bn), lambda j, i: (i, j)),
        out_shape=jax.ShapeDtypeStruct((t, f), BF16),
        scratch_shapes=[pltpu.VMEM((d, bn), BF16), pltpu.VMEM((d, bn), BF16)],
        compiler_params=_cparams(("arbitrary", "arbitrary"), vmem // MIB + 6),
        name="swiglu_up",
    )(x, w1, w3)


def _softplus(x):
    return jnp.maximum(x, 0.0) + jnp.log(1.0 + jnp.exp(-jnp.abs(x)))


def _bf16_terms(v):
    a = v.astype(BF16)
    r = v - a.astype(F32)
    b = r.astype(BF16)
    return a, b, (r - b.astype(F32)).astype(BF16)


def _dn_gate_body(x_ref, wbt_ref, wat_ref, alog_r_ref, dtb_r_ref, alog_c_ref, dtb_c_ref,
                  beta_ref, gam_ref, gamt_ref, w_ref, *, tb):
    x = x_ref[...]
    nt = (((1,), (1,)), ((), ()))

    @pl.when(pl.program_id(0) == 0)
    def _():
        w_ref[...] = jnp.concatenate([wbt_ref[...], wat_ref[...]], axis=0).astype(BF16)

    p = lax.dot_general(x, w_ref[...], nt, preferred_element_type=F32)
    beta_ref[...] = jax.nn.sigmoid(p[:, :LANES])
    pa = p[:, LANES:]
    g = -jnp.exp(alog_r_ref[...]) * _softplus(pa + dtb_r_ref[...])
    pt = pa.T[:DN_HEADS, :]
    gt = -jnp.exp(alog_c_ref[...]) * _softplus(pt + dtb_c_ref[...])
    r = lax.broadcasted_iota(I32, (tb, tb), 0)
    c = lax.broadcasted_iota(I32, (tb, tb), 1)
    same = (r // CHUNK) == (c // CHUNK)
    lower = jnp.where(same & (r >= c), 1.0, 0.0).astype(BF16)
    upper = jnp.where(same & (r <= c), 1.0, 0.0).astype(BF16)
    gam_ref[...] = sum(jnp.dot(lower, term, preferred_element_type=F32) for term in _bf16_terms(g))
    gamt_ref[...] = sum(jnp.dot(term, upper, preferred_element_type=F32) for term in _bf16_terms(gt))


def dn_gates(x, wbt, wat, a_log, dt_bias, *, tb=512):
    t, d = x.shape
    tb = _pick(t, tb)
    pad = lambda v: jnp.zeros((1, LANES), F32).at[0, :DN_HEADS].set(v.astype(F32))
    col = lambda v: v.astype(F32).reshape(DN_HEADS, 1)
    full = lambda shape: pl.BlockSpec(shape, lambda i: (0, 0))
    return pl.pallas_call(
        functools.partial(_dn_gate_body, tb=tb),
        grid=(t // tb,),
        in_specs=[pl.BlockSpec((tb, d), lambda i: (i, 0)),
                  full((LANES, d)), full((LANES, d)),
                  full((1, LANES)), full((1, LANES)), full((DN_HEADS, 1)), full((DN_HEADS, 1))],
        out_specs=[pl.BlockSpec((tb, LANES), lambda i: (i, 0)),
                   pl.BlockSpec((tb, LANES), lambda i: (i, 0)),
                   pl.BlockSpec((DN_HEADS, tb), lambda i: (0, i))],
        out_shape=[jax.ShapeDtypeStruct((t, LANES), F32),
                   jax.ShapeDtypeStruct((t, LANES), F32),
                   jax.ShapeDtypeStruct((DN_HEADS, t), F32)],
        scratch_shapes=[pltpu.VMEM((2 * LANES, d), BF16)],
        compiler_params=_cparams(("arbitrary",), 32),
        name="dn_gates",
    )(x, wbt, wat, pad(a_log), pad(dt_bias), col(a_log), col(dt_bias))


def _mm16(a, b):
    return jnp.dot(a.astype(BF16), b.astype(BF16), preferred_element_type=F32)


def _mm16_each(xs, ys):
    return [_mm16(x, y) for x, y in zip(xs, ys)]


def _unit_lower_inverse_each(a_list, eye, diag_blocks):
    d = [jnp.where(diag_blocks, a, 0.0) for a in a_list]
    n = [a - di for a, di in zip(a_list, d)]
    d2 = _mm16_each(d, d)
    x = _mm16_each([eye - t for t in d], [eye + t for t in d2])
    d4 = _mm16_each(d2, d2)
    x = _mm16_each(x, [eye + t for t in d4])
    d8 = _mm16_each(d4, d4)
    x = _mm16_each(x, [eye + t for t in d8])
    m = _mm16_each(x, n)
    m2 = _mm16_each(m, m)
    y = _mm16_each([eye - t for t in m], [eye + t for t in m2])
    m4 = _mm16_each(m2, m2)
    y = _mm16_each(y, [eye + t for t in m4])
    return _mm16_each(y, x)


def _delta_body(q_ref, k_ref, v_ref, z_ref, beta_ref, gam_ref, gamt_ref, nw_ref, o_ref, s_ref, *,
                hb, tb):
    hg = pl.program_id(0)
    t = pl.program_id(1)

    @pl.when(t == 0)
    def _():
        s_ref[...] = jnp.zeros_like(s_ref)

    row = lax.broadcasted_iota(I32, (CHUNK, CHUNK), 0)
    col = lax.broadcasted_iota(I32, (CHUNK, CHUNK), 1)
    causal = row >= col
    strict = row > col
    diag_blocks = (row // 16) == (col // 16)
    eye = jnp.where(row == col, 1.0, 0.0).astype(F32)
    lane = lax.broadcasted_iota(I32, (tb, LANES), 1)
    nw = nw_ref[...]
    heads = range(hb)
    sq = (CHUNK, CHUNK)

    bcol, gcol, grow = [], [], []
    for hl in heads:
        head = hg * hb + hl
        sel = lane == head
        bcol.append(jnp.sum(jnp.where(sel, beta_ref[...], 0.0), axis=-1, keepdims=True))
        gcol.append(jnp.sum(jnp.where(sel, gam_ref[...], 0.0), axis=-1, keepdims=True))
        grow.append(gamt_ref[pl.ds(head, 1), :])

    for c in range(tb // CHUNK):
        rs = slice(c * CHUNK, (c + 1) * CHUNK)
        cs = [slice(hl * HEAD_DIM, (hl + 1) * HEAD_DIM) for hl in heads]
        q = [q_ref[rs, cs[h]].astype(F32) for h in heads]
        k = [k_ref[rs, cs[h]].astype(F32) for h in heads]
        v = [v_ref[rs, cs[h]].astype(F32) for h in heads]
        bc = [jnp.broadcast_to(bcol[h][rs, :], sq) for h in heads]
        gc = [jnp.broadcast_to(gcol[h][rs, :], sq) for h in heads]
        gr = [jnp.broadcast_to(grow[h][:, rs], sq) for h in heads]
        g_last = [jnp.broadcast_to(grow[h][:, (c + 1) * CHUNK - 1:(c + 1) * CHUNK], sq) for h in heads]
        decay = [jnp.exp(jnp.where(causal, gc[h] - gr[h], NEG_INF)) for h in heads]
        egc = [jnp.exp(gc[h]) for h in heads]
        kt = [k[h].T for h in heads]
        gram = _mm16_each([jnp.concatenate([q[h], k[h]], axis=0) for h in heads], kt)
        a_qk = [gram[h][:CHUNK] * decay[h] for h in heads]
        a_kk = [jnp.where(strict, bc[h] * gram[h][CHUNK:] * decay[h], 0.0) for h in heads]
        tinv = _unit_lower_inverse_each(a_kk, eye, diag_blocks)
        sol = _mm16_each(tinv, [jnp.concatenate([v[h] * bc[h], k[h] * (bc[h] * egc[h])], axis=1)
                                for h in heads])
        lhs1 = [jnp.concatenate([sol[h][:, HEAD_DIM:], q[h] * egc[h]], axis=0) for h in heads]
        kdt = [kt[h] * jnp.exp(g_last[h] - gr[h]) for h in heads]
        s = [s_ref[h] for h in heads]
        ws_qs = _mm16_each(lhs1, s)
        v_new = [sol[h][:, :HEAD_DIM] - ws_qs[h][:CHUNK] for h in heads]
        av_ds = _mm16_each([jnp.concatenate([a_qk[h], kdt[h]], axis=0) for h in heads], v_new)
        for h in heads:
            s_ref[h] = s[h] * jnp.exp(g_last[h]) + av_ds[h][CHUNK:]
            o = ws_qs[h][CHUNK:] + av_ds[h][:CHUNK]
            o = o * lax.rsqrt(jnp.mean(o * o, -1, keepdims=True) + 1e-6) * nw
            zz = z_ref[rs, cs[h]].astype(F32)
            o_ref[rs, cs[h]] = (o * (zz * jax.nn.sigmoid(zz))).astype(o_ref.dtype)


def delta_net(qkv, z, beta, gam, gamt, norm_w, *, hb=16, tb=256):
    t = qkv.shape[0]
    tb = _pick(t, tb)
    w = hb * HEAD_DIM
    pspec = lambda off: pl.BlockSpec((tb, w), lambda h, i: (i, off // w + h))
    return pl.pallas_call(
        functools.partial(_delta_body, hb=hb, tb=tb),
        grid=(DN_HEADS // hb, t // tb),
        in_specs=[pspec(Q_OFF), pspec(K_OFF), pspec(V_OFF), pspec(0),
                  pl.BlockSpec((tb, LANES), lambda h, i: (i, 0)),
                  pl.BlockSpec((tb, LANES), lambda h, i: (i, 0)),
                  pl.BlockSpec((DN_HEADS, tb), lambda h, i: (0, i)),
                  pl.BlockSpec((1, HEAD_DIM), lambda h, i: (0, 0))],
        out_specs=pl.BlockSpec((tb, w), lambda h, i: (i, h)),
        out_shape=jax.ShapeDtypeStruct((t, DN_V_W), BF16),
        scratch_shapes=[pltpu.VMEM((hb, HEAD_DIM, HEAD_DIM), F32)],
        compiler_params=_cparams(("arbitrary", "arbitrary"), 48),
        name="delta_net",
    )(qkv, qkv, qkv, z, beta, gam, gamt, norm_w.astype(F32).reshape(1, HEAD_DIM))


_SWA_QB = 4


def _swa_body(sink_ref, q_ref, kc_ref, kp_ref, vc_ref, vp_ref, o_ref):
    i = pl.program_id(0)
    qi = lax.broadcasted_iota(I32, (WINDOW, 2 * WINDOW), 0)
    kj = lax.broadcasted_iota(I32, (WINDOW, 2 * WINDOW), 1)
    dist = qi - kj + WINDOW
    in_window = (dist >= 0) & (dist < WINDOW)
    first_key = jnp.where(i > 0, 0, WINDOW)
    valid_first = in_window & (kj >= first_key)
    distf = dist.astype(F32)
    scale = HEAD_DIM ** -0.5
    for h in range(SWA_KV_HEADS):
        hs = slice(h * HEAD_DIM, (h + 1) * HEAD_DIM)
        for b in range(_SWA_QB):
            rows = slice(b * WINDOW, (b + 1) * WINDOW)
            if b == 0:
                kk = jnp.concatenate([kp_ref[:, hs], kc_ref[rows, hs]], axis=0)
                vv = jnp.concatenate([vp_ref[:, hs], vc_ref[rows, hs]], axis=0)
                valid = valid_first
            else:
                kk = kc_ref[(b - 1) * WINDOW:(b + 1) * WINDOW, hs]
                vv = vc_ref[(b - 1) * WINDOW:(b + 1) * WINDOW, hs]
                valid = in_window
            vv1 = jnp.concatenate([vv, jnp.ones_like(vv)], axis=1)
            for g in range(SWA_GROUP):
                hq = h * SWA_GROUP + g
                slope = 2.0 ** (-8.0 * (hq + 1) / SWA_Q_HEADS)
                qs = slice(hq * HEAD_DIM, (hq + 1) * HEAD_DIM)
                s = lax.dot_general(q_ref[rows, qs], kk, (((1,), (1,)), ((), ())),
                                    preferred_element_type=F32) * scale
                logits = jnp.where(valid, s - slope * distf, NEG_INF)
                sink = sink_ref[hq]
                m = jnp.maximum(jnp.max(logits, -1, keepdims=True), sink)
                e = jnp.exp(logits - m).astype(BF16)
                ov = jnp.dot(e, vv1, preferred_element_type=F32)
                denom = ov[:, HEAD_DIM:] + jnp.exp(sink - m)
                o_ref[rows, qs] = (ov[:, :HEAD_DIM] / denom).astype(o_ref.dtype)


def swa_attention(proj_sw, sinks):
    t = proj_sw.shape[0]
    qrows = _SWA_QB * WINDOW
    assert t % qrows == 0
    kb, vb = SWA_Q_W // SWA_KV_W, SWA_Q_W // SWA_KV_W + 1
    cur = lambda blk: pl.BlockSpec((qrows, SWA_KV_W), lambda i: (i, blk))
    prev = lambda blk: pl.BlockSpec((WINDOW, SWA_KV_W), lambda i: (jnp.maximum(_SWA_QB * i - 1, 0), blk))
    return pl.pallas_call(
        _swa_body,
        grid=(t // qrows,),
        in_specs=[pl.BlockSpec(memory_space=pltpu.SMEM),
                  pl.BlockSpec((qrows, SWA_Q_W), lambda i: (i, 0)),
                  cur(kb), prev(kb), cur(vb), prev(vb)],
        out_specs=pl.BlockSpec((qrows, SWA_Q_W), lambda i: (i, 0)),
        out_shape=jax.ShapeDtypeStruct((t, SWA_Q_W), BF16),
        compiler_params=_cparams(("parallel",), 32),
        name="swa",
    )(sinks.astype(F32), proj_sw, proj_sw, proj_sw, proj_sw, proj_sw)


def _top2_gates(x, w):
    logits = jnp.dot(x, w, precision=lax.Precision.HIGHEST, preferred_element_type=F32)
    lane = lax.broadcasted_iota(I32, logits.shape, 1)
    neg = jnp.float32(-jnp.inf)
    l1 = jnp.where(lane < N_EXPERTS, logits, neg)
    m1 = jnp.max(l1, -1, keepdims=True)
    i1 = jnp.min(jnp.where(l1 == m1, lane, LANES), -1, keepdims=True)
    l2 = jnp.where(lane == i1, neg, l1)
    m2 = jnp.max(l2, -1, keepdims=True)
    i2 = jnp.min(jnp.where(l2 == m2, lane, LANES), -1, keepdims=True)
    e = jnp.exp(m2 - m1)
    w1 = 1.0 / (1.0 + e)
    w2 = e * w1
    idx = jnp.where(lane == 0, i1, jnp.where(lane == 1, i2, 0))
    gates = jnp.where(lane == 0, w1, jnp.where(lane == 1, w2, 0.0))
    return idx, gates


def _add_ln_router_body(x_ref, y_ref, g_ref, b_ref, rw_ref, of_ref, idx_ref, gate_ref, *, alpha):
    r = alpha * x_ref[...] + y_ref[...].astype(F32)
    o = _layer_norm_rows(r, g_ref[...], b_ref[...])
    of_ref[...] = o
    idx_ref[...], gate_ref[...] = _top2_gates(o, rw_ref[...])


def add_layer_norm_router(x, y, g, b, alpha, router_w, *, bm=256):
    t, d = x.shape
    bm = _pick(t, bm)
    w = jnp.zeros((d, LANES), F32).at[:, :N_EXPERTS].set(router_w.astype(F32))
    row = lambda width: pl.BlockSpec((bm, width), lambda i: (i, 0))
    vec = pl.BlockSpec((1, d), lambda i: (0, 0))
    vmem_mib = (2 * bm * d * (4 + y.dtype.itemsize + 4) + 2 * d * LANES * 4 + 6 * bm * d * 4) // MIB + 8
    return pl.pallas_call(
        functools.partial(_add_ln_router_body, alpha=alpha),
        grid=(t // bm,),
        in_specs=[row(d), row(d), vec, vec, pl.BlockSpec((d, LANES), lambda i: (0, 0))],
        out_specs=[row(d), row(LANES), row(LANES)],
        out_shape=[jax.ShapeDtypeStruct((t, d), F32),
                   jax.ShapeDtypeStruct((t, LANES), I32), jax.ShapeDtypeStruct((t, LANES), F32)],
        compiler_params=_cparams(("parallel",), vmem_mib),
        name="add_ln_router",
    )(x, y, g.reshape(1, d), b.reshape(1, d), w)


def _route_plan(top_i, tm):
    t = top_i.shape[0]
    n_assign = 2 * t
    e_flat = top_i.reshape(-1)
    onehot = (e_flat[:, None] == jnp.arange(N_EXPERTS, dtype=I32)[None, :]).astype(I32)
    csum = jnp.cumsum(onehot, axis=0)
    rank = jnp.sum(csum * onehot, axis=1) - 1
    counts = csum[-1]
    padded = ((counts + tm - 1) // tm) * tm
    ends = jnp.cumsum(padded)
    offs = ends - padded
    pos = offs[e_flat] + rank
    n_tiles = n_assign // tm + N_EXPERTS
    row_src = jnp.zeros((n_tiles * tm,), I32).at[pos].set(jnp.arange(n_assign, dtype=I32) // 2)
    tile_start = jnp.arange(n_tiles, dtype=I32) * tm
    tile_e = jnp.sum((tile_start[:, None] >= ends[None, :]).astype(I32), axis=1)
    tile_valid = (tile_start < ends[-1]).astype(I32)
    last_e = jnp.max(jnp.where(padded > 0, jnp.arange(N_EXPERTS, dtype=I32), 0))
    tile_e = jnp.minimum(tile_e, last_e)
    return row_src, pos[0::2], pos[1::2], tile_e, tile_valid, n_tiles


_ISSUE_UNROLL = 8


def _gather_rows_body(src_ref, x_hbm, o_ref, buf, sem, *, rows):
    i = pl.program_id(0)
    slot = lax.rem(i, 2)

    def row_copy(step_slot, r, src_row):
        return pltpu.make_async_copy(x_hbm.at[pl.ds(src_row, 1), :],
                                     buf.at[step_slot, pl.ds(r, 1), :], sem.at[step_slot])

    def issue(step, step_slot):
        def body(r, carry):
            row_copy(step_slot, r, src_ref[step * rows + r]).start()
            return carry
        lax.fori_loop(0, rows, body, 0, unroll=_ISSUE_UNROLL)

    @pl.when(i == 0)
    def _():
        issue(0, 0)

    @pl.when(i + 1 < pl.num_programs(0))
    def _():
        issue(i + 1, 1 - slot)

    def drain(r, carry):
        row_copy(slot, r, 0).wait()
        return carry

    lax.fori_loop(0, rows, drain, 0, unroll=_ISSUE_UNROLL)
    o_ref[...] = buf[slot].astype(o_ref.dtype)


def gather_rows(x, row_src, *, rows=512):
    d = x.shape[1]
    p = row_src.shape[0]
    rows = _pick(p, rows)
    return pl.pallas_call(
        functools.partial(_gather_rows_body, rows=rows),
        grid_spec=pltpu.PrefetchScalarGridSpec(
            num_scalar_prefetch=1,
            grid=(p // rows,),
            in_specs=[pl.BlockSpec(memory_space=pl.ANY)],
            out_specs=pl.BlockSpec((rows, d), lambda i, src: (i, 0)),
            scratch_shapes=[pltpu.VMEM((2, rows, d), F32), pltpu.SemaphoreType.DMA((2,))]),
        out_shape=jax.ShapeDtypeStruct((p, d), BF16),
        compiler_params=_cparams(("arbitrary",), (rows * d * (2 * 4 + 2 * 2 + 4)) // MIB + 6),
        name="moe_gather",
    )(row_src, x)


def _new_panel(te_ref, i):
    return (i == 0) | (te_ref[i] != te_ref[jnp.maximum(i - 1, 0)])


def _expert_up_body(te_ref, tv_ref, x_ref, w1_ref, w3_ref, h_ref, w1b_ref, w3b_ref):
    i = pl.program_id(1)

    @pl.when(_new_panel(te_ref, i))
    def _():
        w1b_ref[...] = w1_ref[...].astype(BF16)
        w3b_ref[...] = w3_ref[...].astype(BF16)

    @pl.when(tv_ref[i] == 1)
    def _():
        x = x_ref[...]
        a = jnp.dot(x, w1b_ref[...], preferred_element_type=F32)
        b = jnp.dot(x, w3b_ref[...], preferred_element_type=F32)
        h_ref[...] = (a * jax.nn.sigmoid(a) * b).astype(h_ref.dtype)

    @pl.when(tv_ref[i] == 0)
    def _():
        h_ref[...] = jnp.zeros_like(h_ref)


def expert_up(xs, w1, w3, layer, tile_e, tile_valid, tm, *, bn=512):
    p, d = xs.shape
    f = w1.shape[3]
    bn = _pick(f, bn)
    wspec = pl.BlockSpec((None, None, d, bn), lambda j, i, te, tv: (layer, te[i], 0, j))
    vmem = 2 * d * bn * (2 * 4 + 2) + 2 * (tm * d * 2 + tm * bn * 2) + 3 * tm * bn * 4
    return pl.pallas_call(
        _expert_up_body,
        grid_spec=pltpu.PrefetchScalarGridSpec(
            num_scalar_prefetch=2,
            grid=(f // bn, p // tm),
            in_specs=[pl.BlockSpec((tm, d), lambda j, i, te, tv: (i, 0)), wspec, wspec],
            out_specs=pl.BlockSpec((tm, bn), lambda j, i, te, tv: (i, j)),
            scratch_shapes=[pltpu.VMEM((d, bn), BF16), pltpu.VMEM((d, bn), BF16)]),
        out_shape=jax.ShapeDtypeStruct((p, f), BF16),
        compiler_params=_cparams(("arbitrary", "arbitrary"), vmem // MIB + 6),
        name="expert_up",
    )(tile_e, tile_valid, xs, w1, w3)


def _expert_down_body(te_ref, tv_ref, h_ref, w2_ref, y_ref, w2b_ref):
    i = pl.program_id(1)

    @pl.when(_new_panel(te_ref, i))
    def _():
        w2b_ref[...] = w2_ref[...].astype(BF16)

    @pl.when(tv_ref[i] == 1)
    def _():
        y_ref[...] = jnp.dot(h_ref[...], w2b_ref[...], preferred_element_type=F32).astype(y_ref.dtype)

    @pl.when(tv_ref[i] == 0)
    def _():
        y_ref[...] = jnp.zeros_like(y_ref)


def expert_down(h, w2, layer, tile_e, tile_valid, tm, *, bn=1024):
    p, f = h.shape
    d = w2.shape[3]
    bn = _pick(d, bn)
    vmem = f * bn * (2 * 4 + 2) + 2 * (tm * f * 2 + tm * bn * 4) + 2 * tm * bn * 4
    return pl.pallas_call(
        _expert_down_body,
        grid_spec=pltpu.PrefetchScalarGridSpec(
            num_scalar_prefetch=2,
            grid=(d // bn, p // tm),
            in_specs=[pl.BlockSpec((tm, f), lambda j, i, te, tv: (i, 0)),
                      pl.BlockSpec((None, None, f, bn), lambda j, i, te, tv: (layer, te[i], 0, j))],
            out_specs=pl.BlockSpec((tm, bn), lambda j, i, te, tv: (i, j)),
            scratch_shapes=[pltpu.VMEM((f, bn), BF16)]),
        out_shape=jax.ShapeDtypeStruct((p, d), F32),
        compiler_params=_cparams(("arbitrary", "arbitrary"), vmem // MIB + 6),
        name="expert_down",
    )(tile_e, tile_valid, h, w2)


def _combine_ln_body(p1_ref, p2_ref, y_hbm, x_ref, gate_ref, g_ref, b_ref, of_ref, ob_ref,
                     buf, sem, *, rows, alpha):
    i = pl.program_id(0)
    slot = lax.rem(i, 2)

    def row_copy(step_slot, which, r, src_row):
        return pltpu.make_async_copy(y_hbm.at[pl.ds(src_row, 1), :],
                                     buf.at[step_slot, which, pl.ds(r, 1), :], sem.at[step_slot])

    def issue(step, step_slot):
        def body(r, carry):
            row_copy(step_slot, 0, r, p1_ref[step * rows + r]).start()
            row_copy(step_slot, 1, r, p2_ref[step * rows + r]).start()
            return carry
        lax.fori_loop(0, rows, body, 0, unroll=_ISSUE_UNROLL)

    @pl.when(i == 0)
    def _():
        issue(0, 0)

    @pl.when(i + 1 < pl.num_programs(0))
    def _():
        issue(i + 1, 1 - slot)

    def drain(r, carry):
        row_copy(slot, 0, r, 0).wait()
        row_copy(slot, 1, r, 0).wait()
        return carry

    lax.fori_loop(0, rows, drain, 0, unroll=_ISSUE_UNROLL)
    gates = gate_ref[...]
    f = gates[:, 0:1] * buf[slot, 0] + gates[:, 1:2] * buf[slot, 1]
    o = _layer_norm_rows(alpha * x_ref[...] + f, g_ref[...], b_ref[...])
    of_ref[...] = o
    ob_ref[...] = o.astype(BF16)


def moe_combine_ln(y, pos1, pos2, gates, x, g, b, alpha, *, rows=128):
    t, d = x.shape
    rows = _pick(t, rows)
    row = lambda width: pl.BlockSpec((rows, width), lambda i, p1, p2: (i, 0))
    vec = pl.BlockSpec((1, d), lambda i, p1, p2: (0, 0))
    return pl.pallas_call(
        functools.partial(_combine_ln_body, rows=rows, alpha=alpha),
        grid_spec=pltpu.PrefetchScalarGridSpec(
            num_scalar_prefetch=2,
            grid=(t // rows,),
            in_specs=[pl.BlockSpec(memory_space=pl.ANY), row(d), row(LANES), vec, vec],
            out_specs=[row(d), row(d)],
            scratch_shapes=[pltpu.VMEM((2, 2, rows, d), F32), pltpu.SemaphoreType.DMA((2,))]),
        out_shape=[jax.ShapeDtypeStruct((t, d), F32), jax.ShapeDtypeStruct((t, d), BF16)],
        compiler_params=_cparams(("arbitrary",), (rows * d * (16 + 2 * (4 + 4 + 2) + 8)) // MIB + 6),
        name="moe_combine_ln",
    )(pos1, pos2, y, x, gates, g.reshape(1, d), b.reshape(1, d))


def moe_ffn_ln(x_f32, idx, gates, w1, w3, w2, layer, g, b, alpha, *, tm=512):
    row_src, pos1, pos2, tile_e, tile_valid, _ = _route_plan(idx[:, :2], tm)
    xs = gather_rows(x_f32, row_src, rows=tm)
    h = expert_up(xs, w1, w3, layer, tile_e, tile_valid, tm)
    y = expert_down(h, w2, layer, tile_e, tile_valid, tm)
    return moe_combine_ln(y, pos1, pos2, gates, x_f32, g, b, alpha)


def _gate_weights(wt_in, layer):
    d = wt_in.shape[2]
    pad = lambda rows: jnp.zeros((LANES, d), F32).at[:DN_HEADS].set(rows)
    return pad(wt_in[layer, _IN_Z:_IN_B, :]), pad(wt_in[layer, _IN_B:_IN_A, :])


def kernel(x, w_in, conv_w, a_log, dt_bias, dn_norm_w, sinks, w_o, ln1_g, ln1_b, ffn_w1, ffn_w3,
           ffn_w2, router_w, exp_w1, exp_w3, exp_w2, ln2_g, ln2_b):
    bsz, seq, d = x.shape
    depth = w_in.shape[0]
    alpha = (2 * depth) ** 0.25
    w_in, w_o, conv_w = w_in.astype(F32), w_o.astype(F32), conv_w.astype(F32)
    wt_in = jnp.swapaxes(w_in, 1, 2)
    wt_sw = wt_in[:, _IN_A:, :]
    outs = []
    for bi in range(bsz):
        xf = x[bi].astype(F32)
        xb = xf.astype(BF16)
        for i in range(depth):
            w_bt, w_at = _gate_weights(wt_in, i)
            qkv = in_proj(xb, wt_in, i, n_cols=_IN_QKV, conv_w=conv_w, name="in_proj_qkv")
            z = in_proj(xb, wt_in, i, row0=_IN_QKV, n_cols=DN_V_W, name="in_proj_z")
            proj_sw = in_proj(xb, wt_sw, i, name="in_proj_sw")
            beta, gam, gamt = dn_gates(xb, w_bt, w_at, a_log[i], dt_bias[i])
            out_a = delta_net(qkv, z, beta, gam, gamt, dn_norm_w[i])
            out_b = swa_attention(proj_sw, sinks[i])
            mix = matmul_ws([out_a, out_b], w_o, i, BF16, name="out_proj")
            j = i // 2
            if i % 2 == 0:
                xf, xb = add_layer_norm(xf, mix, ln1_g[i], ln1_b[i], alpha)
                h = swiglu_up(xb, ffn_w1.astype(F32), ffn_w3.astype(F32), j)
                f = matmul_ws([h], ffn_w2.astype(F32), j, BF16, bm=512, bn=512, name="ffn_down")
                xf, xb = add_layer_norm(xf, f, ln2_g[i], ln2_b[i], alpha)
            else:
                xf, idx, gates = add_layer_norm_router(xf, mix, ln1_g[i], ln1_b[i], alpha, router_w[j])
                xf, xb = moe_ffn_ln(xf, idx, gates, exp_w1.astype(F32), exp_w3.astype(F32),
                                    exp_w2.astype(F32), j, ln2_g[i], ln2_b[i], alpha)
        outs.append(xf)
    return jnp.stack(outs, axis=0).astype(x.dtype)
```
